```python
import math
import jax, jax.numpy as jnp
from jax import lax
import numpy as np

D_MODEL = 4096
BATCH = 4
SEQ = 4096
DEPTH = 4

N_MIXERS = 2
N_RET = (DEPTH + N_MIXERS - 1) // N_MIXERS
N_RWKV = DEPTH // N_MIXERS
N_VRES = N_RWKV - 1

RET_HEADS = 16
RET_HEAD_DIM = D_MODEL // RET_HEADS
RET_CHUNK = 128
ROPE_BASE = 10000.0
RET_GN_EPS = 1e-5

RWKV_HEAD_DIM = 64
RWKV_HEADS = D_MODEL // RWKV_HEAD_DIM
RWKV_GN_EPS = 64e-5
DECAY_LORA = max(32, int(round(1.8 * D_MODEL ** 0.5 / 32)) * 32)
AAA_LORA = max(32, int(round(1.8 * D_MODEL ** 0.5 / 32)) * 32)
MV_LORA = max(32, int(round(1.3 * D_MODEL ** 0.5 / 32)) * 32)
GATE_LORA = max(32, int(round(0.6 * D_MODEL ** 0.8 / 32)) * 32)

D_FF = 4 * D_MODEL
LN_EPS = 1e-5
ALPHA = (2 * DEPTH) ** 0.25
BETA = (8 * DEPTH) ** -0.25

kernel_name = "retnet_rwkv7_interleaved_deepnorm"


def _layer_norm(x, g, b, eps=LN_EPS):
    xf = x.astype(jnp.float32)
    mu = jnp.mean(xf, axis=-1, keepdims=True)
    var = jnp.mean(jnp.square(xf - mu), axis=-1, keepdims=True)
    return ((xf - mu) * lax.rsqrt(var + eps) * g + b).astype(x.dtype)


def _head_group_norm(y, g, b, eps):
    yf = y.astype(jnp.float32)
    mu = jnp.mean(yf, axis=-1, keepdims=True)
    var = jnp.mean(jnp.square(yf - mu), axis=-1, keepdims=True)
    bn, t, h, d = y.shape
    return ((yf - mu) * lax.rsqrt(var + eps)).reshape(bn, t, h * d) * g + b


def _rotary(t, positions):
    half = t.shape[-1] // 2
    inv_freq = ROPE_BASE ** (-jnp.arange(half, dtype=jnp.float32) / half)
    ang = positions.astype(jnp.float32)[..., None, None] * inv_freq
    cos, sin = jnp.cos(ang), jnp.sin(ang)
    t1 = t[..., :half].astype(jnp.float32)
    t2 = t[..., half:].astype(jnp.float32)
    return jnp.concatenate([t1 * cos - t2 * sin, t1 * sin + t2 * cos], axis=-1)


def _retention_chunkwise(q, k, v):
    bn, t, h, dk = q.shape
    dv = v.shape[-1]
    n_chunks = t // RET_CHUNK
    log_g = jnp.log1p(-(2.0 ** (-5.0 - jnp.arange(h, dtype=jnp.float32))))
    idx = jnp.arange(RET_CHUNK, dtype=jnp.float32)
    diff = idx[:, None] - idx[None, :]
    mask = jnp.where(diff >= 0, jnp.exp(log_g[:, None, None] * jnp.maximum(diff, 0.0)), 0.0)
    q_decay = jnp.exp(log_g[:, None] * (idx + 1.0))
    k_decay = jnp.exp(log_g[:, None] * (RET_CHUNK - 1.0 - idx))
    chunk_decay = jnp.exp(log_g * RET_CHUNK)

    def to_chunks(a):
        return a.reshape(bn, n_chunks, RET_CHUNK, h, a.shape[-1]).transpose(1, 0, 3, 2, 4)

    def step(state, qkv):
        qc, kc, vc = qkv
        inner = jnp.einsum('bhnd,bhmd->bhnm', qc, kc) * mask
        y = (jnp.einsum('bhnm,bhmv->bhnv', inner, vc)
             + jnp.einsum('bhnd,bhdv->bhnv', qc * q_decay[..., None], state))
        state = (state * chunk_decay[:, None, None]
                 + jnp.einsum('bhmd,bhmv->bhdv', kc * k_decay[..., None], vc))
        return state, y

    s0 = jnp.zeros((bn, h, dk, dv), jnp.float32)
    _, y = lax.scan(step, s0, (to_chunks(q), to_chunks(k), to_chunks(v)))
    return y.transpose(1, 0, 3, 2, 4).reshape(bn, t, h, dv)


def retention(x, positions, w_in, gn_g, gn_b, w_o):
    bn, t, _ = x.shape
    q, k, v, g = jnp.split(x @ w_in, 4, axis=-1)
    shp = (bn, t, RET_HEADS, RET_HEAD_DIM)
    q = _rotary(q.reshape(shp), positions)
    k = _rotary(k.reshape(shp), positions) * (RET_HEAD_DIM ** -0.5)
    y = _retention_chunkwise(q, k, v.reshape(shp).astype(jnp.float32))
    y = _head_group_norm(y, gn_g, gn_b, RET_GN_EPS)
    return (jax.nn.silu(g.astype(jnp.float32)) * y).astype(x.dtype) @ w_o


def _wkv7_scan(r, w, k, v, a, b):
    bn, t, h, n = r.shape

    def step(state, inp):
        r_t, w_t, k_t, v_t, a_t, b_t = inp
        sa = jnp.einsum('bhvk,bhk->bhv', state, a_t)
        state = (state * w_t[:, :, None, :] + sa[..., None] * b_t[:, :, None, :]
                 + v_t[..., None] * k_t[:, :, None, :])
        return state, jnp.einsum('bhvk,bhk->bhv', state, r_t)

    s0 = jnp.zeros((bn, h, n, n), jnp.float32)
    xs = tuple(jnp.moveaxis(a, 1, 0) for a in (r, w, k, v, a, b))
    _, y = lax.scan(step, s0, xs)
    return jnp.moveaxis(y, 0, 1)


def rwkv7_time_mix(x, v_first, mu, w_rkv, w0, w1, w2, a0, a1, a2, g1, g2,
                   k_k, k_a, r_k, gn_g, gn_b, w_o, v_res):
    bn, t, d = x.shape
    xx = jnp.pad(x, ((0, 0), (1, 0), (0, 0)))[:, :t] - x
    xr, xw, xk, xv, xa, xg = (x + xx * mu[i] for i in range(6))
    r = xr @ w_rkv[0]
    k = xk @ w_rkv[1]
    v = xv @ w_rkv[2]
    w_log = -jax.nn.softplus(-(w0 + jnp.tanh(xw @ w1) @ w2)) - 0.5
    decay = jnp.exp(-jnp.exp(w_log.astype(jnp.float32)))
    a = jax.nn.sigmoid(a0 + (xa @ a1) @ a2)
    g = jax.nn.sigmoid(xg @ g1) @ g2
    if v_res is None:
        v_first = v
    else:
        v0, v1, v2 = v_res
        v = v + (v_first - v) * jax.nn.sigmoid(v0 + (xv @ v1) @ v2)

    def heads(a_):
        return a_.reshape(bn, t, RWKV_HEADS, RWKV_HEAD_DIM).astype(jnp.float32)

    kk = heads(k * k_k)
    kk = kk * lax.rsqrt(jnp.maximum(jnp.sum(kk * kk, axis=-1, keepdims=True), 1e-24))
    k = k * (1.0 + (a - 1.0) * k_a)
    rh, kh, vh, ah = heads(r), heads(k), heads(v), heads(a)
    y = _wkv7_scan(rh, heads(decay), kh, vh, -kk, kk * ah)
    y = _head_group_norm(y, gn_g, gn_b, RWKV_GN_EPS)
    bonus = jnp.sum(rh * kh * r_k, axis=-1, keepdims=True) * vh
    y = y + bonus.reshape(bn, t, d)
    return (y * g).astype(x.dtype) @ w_o, v_first


def squared_relu_mlp(x, w1, w2):
    return jnp.square(jax.nn.relu(x @ w1)) @ w2


def setup_inputs(seed: int = 0) -> dict:
    key = jax.random.key(seed)
    ks = iter(jax.random.split(key, 40))
    D, H, N = D_MODEL, RWKV_HEADS, RWKV_HEAD_DIM

    def nrm(shape, scale):
        return scale * jax.random.normal(next(ks), shape, jnp.float32)

    def unif(shape, lo, hi):
        return jax.random.uniform(next(ks), shape, jnp.float32, lo, hi)

    x = nrm((BATCH, SEQ, D), 1.0)
    offset = jax.random.randint(next(ks), (BATCH, 1), 0, 1024, jnp.int32)
    positions = offset + jnp.arange(SEQ, dtype=jnp.int32)[None, :]
    return {
        "x": x,
        "positions": positions,
        "ret_w_in": nrm((N_RET, D, 4 * D), D ** -0.5),
        "ret_gn_g": 1.0 + nrm((N_RET, D), 0.02),
        "ret_gn_b": nrm((N_RET, D), 0.02),
        "ret_w_o": nrm((N_RET, D, D), BETA * D ** -0.5),
        "rwkv_mu": unif((N_RWKV, 6, D), 0.0, 1.0),
        "rwkv_w_rkv": nrm((N_RWKV, 3, D, D), D ** -0.5),
        "rwkv_w0": unif((N_RWKV, D), -6.0, -1.0),
        "rwkv_w1": nrm((N_RWKV, D, DECAY_LORA), D ** -0.5),
        "rwkv_w2": nrm((N_RWKV, DECAY_LORA, D), 0.1 * DECAY_LORA ** -0.5),
        "rwkv_a0": nrm((N_RWKV, D), 0.1),
        "rwkv_a1": nrm((N_RWKV, D, AAA_LORA), D ** -0.5),
        "rwkv_a2": nrm((N_RWKV, AAA_LORA, D), 0.1 * AAA_LORA ** -0.5),
        "rwkv_g1": nrm((N_RWKV, D, GATE_LORA), D ** -0.5),
        "rwkv_g2": nrm((N_RWKV, GATE_LORA, D), GATE_LORA ** -0.5),
        "rwkv_k_k": 0.85 + nrm((N_RWKV, D), 0.02),
        "rwkv_k_a": 1.0 + nrm((N_RWKV, D), 0.02),
        "rwkv_r_k": nrm((N_RWKV, H, N), 0.1),
        "rwkv_gn_g": 1.0 + nrm((N_RWKV, D), 0.02),
        "rwkv_gn_b": nrm((N_RWKV, D), 0.02),
        "rwkv_w_o": nrm((N_RWKV, D, D), BETA * D ** -0.5),
        "rwkv_v0": 1.0 + nrm((N_VRES, D), 0.02),
        "rwkv_v1": nrm((N_VRES, D, MV_LORA), D ** -0.5),
        "rwkv_v2": nrm((N_VRES, MV_LORA, D), 0.1 * MV_LORA ** -0.5),
        "ln_mix_g": 1.0 + nrm((DEPTH, D), 0.02),
        "ln_mix_b": nrm((DEPTH, D), 0.02),
        "mlp_w1": nrm((DEPTH, D, D_FF), D ** -0.5),
        "mlp_w2": nrm((DEPTH, D_FF, D), BETA * D_FF ** -0.5),
        "ln_mlp_g": 1.0 + nrm((DEPTH, D), 0.02),
        "ln_mlp_b": nrm((DEPTH, D), 0.02),
    }


def reference(x, positions, ret_w_in, ret_gn_g, ret_gn_b, ret_w_o,
              rwkv_mu, rwkv_w_rkv, rwkv_w0, rwkv_w1, rwkv_w2, rwkv_a0, rwkv_a1, rwkv_a2,
              rwkv_g1, rwkv_g2, rwkv_k_k, rwkv_k_a, rwkv_r_k, rwkv_gn_g, rwkv_gn_b, rwkv_w_o,
              rwkv_v0, rwkv_v1, rwkv_v2,
              ln_mix_g, ln_mix_b, mlp_w1, mlp_w2, ln_mlp_g, ln_mlp_b):
    v_first = None
    for i in range(DEPTH):
        j = i // N_MIXERS
        if i % N_MIXERS == 0:
            h = retention(x, positions, ret_w_in[j], ret_gn_g[j], ret_gn_b[j], ret_w_o[j])
        else:
            v_res = None if j == 0 else (rwkv_v0[j - 1], rwkv_v1[j - 1], rwkv_v2[j - 1])
            h, v_first = rwkv7_time_mix(
                x, v_first, rwkv_mu[j], rwkv_w_rkv[j], rwkv_w0[j], rwkv_w1[j], rwkv_w2[j],
                rwkv_a0[j], rwkv_a1[j], rwkv_a2[j], rwkv_g1[j], rwkv_g2[j],
                rwkv_k_k[j], rwkv_k_a[j], rwkv_r_k[j], rwkv_gn_g[j], rwkv_gn_b[j], rwkv_w_o[j],
                v_res)
        x = _layer_norm(ALPHA * x + h, ln_mix_g[i], ln_mix_b[i])
        x = _layer_norm(ALPHA * x + squared_relu_mlp(x, mlp_w1[i], mlp_w2[i]),
                        ln_mlp_g[i], ln_mlp_b[i])
    return x
```

```python
import functools

import jax
import jax.numpy as jnp
from jax import lax
from jax.experimental import pallas as pl
from jax.experimental.pallas import tpu as pltpu

F32 = jnp.float32
BF16 = jnp.bfloat16

RET_HEAD_DIM = 256
RET_CHUNK = 128
ROPE_BASE = 10000.0
RET_GN_EPS = 1e-5
RWKV_HEAD_DIM = 64
RWKV_GN_EPS = 64e-5
LN_EPS = 1e-5
N_MIXERS = 2

LANES = 128
MXU_DIM = 256
WKV_CHUNK = 64
WKV_GROUP = MXU_DIM // RWKV_HEAD_DIM
VMEM_LIMIT = 56 * 1024 * 1024


def _cparams(sem, vmem=None):
    return pltpu.CompilerParams(dimension_semantics=sem, vmem_limit_bytes=vmem)


def _act(x, act):
    if act == "relu2":
        return jnp.square(jnp.maximum(x, 0.0))
    return x


def _mm_kernel(x_ref, w_ref, o_ref, *scratch, nk, act):
    part = jnp.dot(x_ref[...], w_ref[...], preferred_element_type=F32)
    if nk == 1:
        o_ref[...] = _act(part, act).astype(o_ref.dtype)
        return
    acc_ref, = scratch
    k = pl.program_id(3)

    @pl.when(k == 0)
    def _():
        acc_ref[...] = part

    @pl.when(k > 0)
    def _():
        acc_ref[...] += part

    @pl.when(k == nk - 1)
    def _():
        o_ref[...] = _act(acc_ref[...], act).astype(o_ref.dtype)


def _matmul(x, w, *, act=None, out_dtype=BF16):
    _, m, kd = x.shape
    g, _, n = w.shape
    tm, tn, tk = min(m, 1024), min(n, 1024), min(kd, 4096)
    nk = kd // tk
    scratch = [pltpu.VMEM((tm, tn), F32)] if nk > 1 else []
    return pl.pallas_call(
        functools.partial(_mm_kernel, nk=nk, act=act),
        grid=(g, m // tm, n // tn, nk),
        in_specs=[pl.BlockSpec((None, tm, tk), lambda b, i, j, k: (b, i, k)),
                  pl.BlockSpec((None, tk, tn), lambda b, i, j, k: (b, k, j))],
        out_specs=pl.BlockSpec((None, tm, tn), lambda b, i, j, k: (b, i, j)),
        out_shape=jax.ShapeDtypeStruct((g, m, n), out_dtype),
        scratch_shapes=scratch,
        compiler_params=_cparams(("parallel", "parallel", "parallel", "arbitrary"), VMEM_LIMIT),
        name="matmul",
    )(x, w)


def _mm2d(x, w, **kw):
    return _matmul(x[None], w[None], **kw)[0]


def _add_ln_kernel(x_ref, h_ref, g_ref, b_ref, of_ref, ob_ref, *, alpha):
    z = alpha * x_ref[...] + h_ref[...].astype(F32)
    mu = jnp.mean(z, axis=-1, keepdims=True)
    zc = z - mu
    var = jnp.mean(zc * zc, axis=-1, keepdims=True)
    y = zc * lax.rsqrt(var + LN_EPS) * g_ref[...] + b_ref[...]
    of_ref[...] = y
    ob_ref[...] = y.astype(BF16)


def _add_ln(x, h, g, b, alpha):
    m, d = x.shape
    tm = min(m, 256)
    row = pl.BlockSpec((tm, d), lambda i: (i, 0))
    vec = pl.BlockSpec((1, d), lambda i: (0, 0))
    return pl.pallas_call(
        functools.partial(_add_ln_kernel, alpha=alpha),
        grid=(m // tm,),
        in_specs=[row, row, vec, vec],
        out_specs=[row, row],
        out_shape=[jax.ShapeDtypeStruct((m, d), F32), jax.ShapeDtypeStruct((m, d), BF16)],
        compiler_params=_cparams(("parallel",), VMEM_LIMIT),
        name="add_ln",
    )(x, h, g.reshape(1, d), b.reshape(1, d))


def _rope_kernel(pos_ref, freq_ref, cos_ref, sin_ref):
    ang = pos_ref[...].astype(F32) * freq_ref[...]
    cos_ref[...] = jnp.cos(ang)
    sin_ref[...] = jnp.sin(ang)


def _rope_tables(positions):
    m = positions.size
    half = RET_HEAD_DIM // 2
    inv_freq = ROPE_BASE ** (-jnp.arange(half, dtype=F32) / half)
    tm = min(m, 512)
    out = pl.BlockSpec((tm, half), lambda i: (i, 0))
    return pl.pallas_call(
        _rope_kernel,
        grid=(m // tm,),
        in_specs=[pl.BlockSpec((tm, 1), lambda i: (i, 0)),
                  pl.BlockSpec((1, half), lambda i: (0, 0))],
        out_specs=[out, out],
        out_shape=[jax.ShapeDtypeStruct((m, half), F32)] * 2,
        compiler_params=_cparams(("parallel",)),
        name="rope_tables",
    )(positions.reshape(m, 1), inv_freq.reshape(1, half))


def _ret_kernel(q_ref, k_ref, v_ref, g_ref, cos_ref, sin_ref, mask_ref, qd_ref, kd_ref, cd_ref,
                gg_ref, gb_ref, o_ref, state_ref, *, n_chunks):
    c_len = RET_CHUNK
    half = RET_HEAD_DIM // 2
    state_ref[...] = jnp.zeros_like(state_ref)

    def rot(t, cos, sin):
        t1, t2 = t[:, :half], t[:, half:]
        return jnp.concatenate([t1 * cos - t2 * sin, t1 * sin + t2 * cos], axis=-1)

    def body(c, carry):
        rows = pl.ds(pl.multiple_of(c * c_len, c_len), c_len)
        cos, sin = cos_ref[rows, :], sin_ref[rows, :]
        q = rot(q_ref[rows, :].astype(F32), cos, sin)
        k = rot(k_ref[rows, :].astype(F32), cos, sin) * (RET_HEAD_DIM ** -0.5)
        v = v_ref[rows, :]
        inner = lax.dot_general(q.astype(BF16), k.astype(BF16), (((1,), (1,)), ((), ())),
                                preferred_element_type=F32) * mask_ref[0]
        st = state_ref[...]
        y = (jnp.dot(inner.astype(BF16), v, preferred_element_type=F32)
             + jnp.dot((q * qd_ref[0]).astype(BF16), st.astype(BF16), preferred_element_type=F32))
        kdt = jnp.transpose(k * kd_ref[0]).astype(BF16)
        state_ref[...] = st * cd_ref[0][0:1, :] + jnp.dot(kdt, v, preferred_element_type=F32)
        mu = jnp.mean(y, axis=-1, keepdims=True)
        yc = y - mu
        var = jnp.mean(yc * yc, axis=-1, keepdims=True)
        yn = yc * lax.rsqrt(var + RET_GN_EPS) * gg_ref[...] + gb_ref[...]
        g = g_ref[rows, :].astype(F32)
        o_ref[rows, :] = (g * jax.nn.sigmoid(g) * yn).astype(o_ref.dtype)
        return carry

    lax.fori_loop(0, n_chunks, body, 0)


def _retention_core(qkvg, cos, sin, gn_g, gn_b, bn, t):
    d = qkvg.shape[1] // 4
    h = d // RET_HEAD_DIM
    c_len, dh = RET_CHUNK, RET_HEAD_DIM
    log_g = jnp.log1p(-(2.0 ** (-5.0 - jnp.arange(h, dtype=F32))))
    idx = jnp.arange(c_len, dtype=F32)
    diff = idx[:, None] - idx[None, :]
    mask = jnp.where(diff >= 0, jnp.exp(log_g[:, None, None] * jnp.maximum(diff, 0.0)), 0.0)
    q_decay = jnp.exp(log_g[:, None] * (idx + 1.0))
    k_decay = jnp.exp(log_g[:, None] * (c_len - 1.0 - idx))
    chunk_decay = jnp.exp(log_g * c_len)
    qd = jnp.broadcast_to(q_decay[:, :, None], (h, c_len, dh))
    kd = jnp.broadcast_to(k_decay[:, :, None], (h, c_len, dh))
    cd = jnp.broadcast_to(chunk_decay[:, None, None], (h, 8, dh))

    def col(off):
        return pl.BlockSpec((t, dh), lambda b, j, off=off: (b, off * h + j))

    tab = pl.BlockSpec((t, dh // 2), lambda b, j: (b, 0))
    vec = pl.BlockSpec((1, dh), lambda b, j: (0, j))
    return pl.pallas_call(
        functools.partial(_ret_kernel, n_chunks=t // c_len),
        grid=(bn, h),
        in_specs=[col(0), col(1), col(2), col(3), tab, tab,
                  pl.BlockSpec((1, c_len, c_len), lambda b, j: (j, 0, 0)),
                  pl.BlockSpec((1, c_len, dh), lambda b, j: (j, 0, 0)),
                  pl.BlockSpec((1, c_len, dh), lambda b, j: (j, 0, 0)),
                  pl.BlockSpec((1, 8, dh), lambda b, j: (j, 0, 0)),
                  vec, vec],
        out_specs=pl.BlockSpec((t, dh), lambda b, j: (b, j)),
        out_shape=jax.ShapeDtypeStruct((bn * t, d), BF16),
        scratch_shapes=[pltpu.VMEM((dh, dh), F32)],
        compiler_params=_cparams(("parallel", "parallel"), VMEM_LIMIT),
        name="retention",
    )(qkvg, qkvg, qkvg, qkvg, cos, sin, mask, qd, kd, cd, gn_g.reshape(1, d), gn_b.reshape(1, d))


def _mix_kernel(x_ref, prev_ref, mu_ref, o_ref, *, tiles_per_seq):
    x = x_ref[...]
    first = (pl.program_id(0) % tiles_per_seq) == 0
    prev_row = jnp.where(first, 0.0, prev_ref[7:8, :])
    shifted = pltpu.roll(x, 1, axis=0)
    rid = lax.broadcasted_iota(jnp.int32, x.shape, 0)
    xx = jnp.where(rid == 0, prev_row, shifted) - x
    for i in range(o_ref.shape[0]):
        o_ref[i] = (x + xx * mu_ref[i:i + 1, :]).astype(o_ref.dtype)


def _token_mix(x, mu, t):
    m, d = x.shape
    tm = min(t, 256)
    return pl.pallas_call(
        functools.partial(_mix_kernel, tiles_per_seq=t // tm),
        grid=(m // tm,),
        in_specs=[pl.BlockSpec((tm, d), lambda i: (i, 0)),
                  pl.BlockSpec((8, d), lambda i: (jnp.maximum(i * (tm // 8) - 1, 0), 0)),
                  pl.BlockSpec((6, d), lambda i: (0, 0))],
        out_specs=pl.BlockSpec((6, tm, d), lambda i: (0, i, 0)),
        out_shape=jax.ShapeDtypeStruct((6, m, d), BF16),
        compiler_params=_cparams(("parallel",), VMEM_LIMIT),
        name="token_mix",
    )(x, x, mu)


def _lora_kernel(x_ref, w1_ref, w2_ref, o_ref, *, act):
    t = jnp.dot(x_ref[...], w1_ref[...], preferred_element_type=F32)
    if act == "tanh":
        t = jnp.tanh(t)
    elif act == "sigmoid":
        t = jax.nn.sigmoid(t)
    o_ref[...] = jnp.dot(t.astype(BF16), w2_ref[...], preferred_element_type=F32).astype(o_ref.dtype)


def _lora(x6, idx, w1, w2, act, out_dtype):
    _, m, d = x6.shape
    r = w1.shape[1]
    rp = -(-r // LANES) * LANES
    w1p = jnp.pad(w1, ((0, 0), (0, rp - r))).astype(BF16)
    w2p = jnp.pad(w2, ((0, rp - r), (0, 0))).astype(BF16)
    tm = min(m, 512)
    return pl.pallas_call(
        functools.partial(_lora_kernel, act=act),
        grid=(m // tm,),
        in_specs=[pl.BlockSpec((None, tm, d), lambda i: (idx, i, 0)),
                  pl.BlockSpec((d, rp), lambda i: (0, 0)),
                  pl.BlockSpec((rp, d), lambda i: (0, 0))],
        out_specs=pl.BlockSpec((tm, d), lambda i: (i, 0)),
        out_shape=jax.ShapeDtypeStruct((m, d), out_dtype),
        compiler_params=_cparams(("parallel",), VMEM_LIMIT),
        name="lora",
    )(x6, w1p, w2p)


def _split3(x):
    hi = x.astype(BF16)
    r1 = x - hi.astype(F32)
    mid = r1.astype(BF16)
    lo = (r1 - mid.astype(F32)).astype(BF16)
    return hi, mid, lo


def _block_diag(y, half_masks):
    yb = y.astype(BF16)
    zeros = jnp.zeros((y.shape[0], LANES), BF16)
    blocks = []
    for h in range(WKV_GROUP):
        col = h // 2
        part = yb[:, col * LANES:(col + 1) * LANES] * half_masks[h % 2]
        blocks.append(jnp.concatenate([part, zeros] if col == 0 else [zeros, part], axis=1))
    return jnp.concatenate(blocks, axis=0)


def _wkv_kernel(r_ref, k_ref, v_ref, lw_ref, la_ref, lg_ref, *rest, n_chunks, vres):
    if vres:
        lv_ref, vf_ref = rest[0], rest[1]
        rest = rest[2:]
    (w0_ref, a0_ref, kk_ref, ka_ref, rk_ref, gg_ref, gb_ref, v0_ref,
     ones_ref, tri_ref, lo_ref, bdm_ref, o_ref, state_ref) = rest
    L = WKV_CHUNK

    @pl.when(pl.program_id(2) == 0)
    def _():
        state_ref[...] = jnp.zeros_like(state_ref)

    ones_bd = ones_ref[...]
    tri = tri_ref[...]
    lane = lax.broadcasted_iota(jnp.int32, (1, LANES), 1)
    half_masks = [(lane < RWKV_HEAD_DIM).astype(BF16), (lane >= RWKV_HEAD_DIM).astype(BF16)]

    def seg_sum(x):
        hi = x.astype(BF16)
        lo = (x - hi.astype(F32)).astype(BF16)
        return (jnp.dot(hi, ones_bd, preferred_element_type=F32)
                + jnp.dot(lo, ones_bd, preferred_element_type=F32))

    def bd(y):
        return _block_diag(y, half_masks)

    def mm(a, b):
        return jnp.dot(a, b, preferred_element_type=F32)

    def body(c, carry):
        rows = pl.ds(pl.multiple_of(c * L, L), L)
        r = r_ref[rows, :].astype(F32)
        k0 = k_ref[rows, :].astype(F32)
        v = v_ref[rows, :].astype(F32)
        z = -(w0_ref[...] + lw_ref[rows, :])
        softplus = jnp.maximum(z, 0.0) + jnp.log1p(jnp.exp(-jnp.abs(z)))
        logw = -jnp.exp(-softplus - 0.5)
        a = jax.nn.sigmoid(a0_ref[...] + la_ref[rows, :].astype(F32))
        if vres:
            v = v + (vf_ref[rows, :].astype(F32) - v) * jax.nn.sigmoid(
                v0_ref[...] + lv_ref[rows, :].astype(F32))
        kk = k0 * kk_ref[...]
        kk = kk * lax.rsqrt(jnp.maximum(seg_sum(kk * kk), 1e-24))
        k = k0 * (1.0 + (a - 1.0) * ka_ref[...])
        avec, bvec = -kk, kk * a

        hi, mid, lo = _split3(logw)
        cum = mm(tri, hi) + mm(tri, mid) + mm(tri, lo)
        cum_l = cum[L - 1:L, :]
        e_out = jnp.exp(-cum)
        tail = jnp.exp(cum_l - cum)
        at = (avec * jnp.exp(cum - logw)).astype(BF16)
        rt = (r * jnp.exp(cum)).astype(BF16)
        ar = jnp.concatenate([at, rt], axis=0)
        bk = jnp.concatenate([bd(bvec * e_out), bd(k * e_out)], axis=0)
        p = lax.dot_general(ar, bk, (((1,), (1,)), ((), ())), preferred_element_type=F32)
        strict, incl = lo_ref[0], lo_ref[1]
        m_ab = p[:L, :MXU_DIM] * strict
        m_ak = p[:L, MXU_DIM:] * strict
        n_ab = p[L:, :MXU_DIM] * incl
        n_ak = p[L:, MXU_DIM:] * incl

        st = state_ref[...]
        ph = lax.dot_general(ar, st.astype(BF16), (((1,), (1,)), ((), ())),
                             preferred_element_type=F32)
        v_bd = bd(v)
        pv = mm(jnp.concatenate([m_ak, n_ak], axis=0).astype(BF16), v_bd)

        u = ph[:L] + pv[:L]
        x = m_ab
        for j in range(6):
            u = u + mm(x.astype(BF16), bd(u))
            if j < 5:
                x = mm(x.astype(BF16), bd(x))

        y = ph[L:] + pv[L:] + mm(n_ab.astype(BF16), bd(u))

        uv = jnp.concatenate([u, v], axis=0).astype(BF16)
        bkh = jnp.concatenate([bvec * tail, k * tail], axis=0).astype(BF16)
        upd = lax.dot_general(uv, bkh, (((0,), (0,)), ((), ())), preferred_element_type=F32)
        state_ref[...] = st * jnp.exp(cum_l) + upd * bdm_ref[...]

        mu = seg_sum(y) * (1.0 / RWKV_HEAD_DIM)
        yc = y - mu
        var = seg_sum(yc * yc) * (1.0 / RWKV_HEAD_DIM)
        yn = yc * lax.rsqrt(var + RWKV_GN_EPS) * gg_ref[...] + gb_ref[...]
        bonus = seg_sum(r * k * rk_ref[...]) * v
        o_ref[rows, :] = ((yn + bonus) * lg_ref[rows, :].astype(F32)).astype(o_ref.dtype)
        return carry

    lax.fori_loop(0, n_chunks, body, 0)


def _wkv_core(rkv, lw, la, lg, lv, v_first, params, bn, t):
    _, m, d = rkv.shape
    L, gw = WKV_CHUNK, MXU_DIM
    rb = min(t, 1024)
    nq = d // gw
    vres = lv is not None
    hid = jnp.arange(gw) // RWKV_HEAD_DIM
    bdm = (hid[:, None] == hid[None, :])
    ones_bd = bdm.astype(BF16)
    ti = jnp.arange(L)
    tri = (ti[:, None] >= ti[None, :]).astype(BF16)
    si = jnp.arange(gw) % L
    lo_masks = jnp.stack([(si[None, :] < ti[:, None]), (si[None, :] <= ti[:, None])]).astype(F32)

    def act(g=None):
        if g is None:
            return pl.BlockSpec((rb, gw), lambda b, q, i: (b * (t // rb) + i, q))
        return pl.BlockSpec((None, rb, gw), lambda b, q, i, g=g: (g, b * (t // rb) + i, q))

    vec = pl.BlockSpec((1, gw), lambda b, q, i: (0, q))

    def const(shape):
        return pl.BlockSpec(shape, lambda b, q, i: (0,) * len(shape))

    ins = [rkv, rkv, rkv, lw, la, lg]
    specs = [act(0), act(1), act(2), act(), act(), act()]
    if vres:
        ins += [lv, v_first]
        specs += [act(), act(2)]
    ins += [p.reshape(1, d) for p in params]
    specs += [vec] * len(params)
    ins += [ones_bd, tri, lo_masks, bdm.astype(F32)]
    specs += [const((gw, gw)), const((L, L)), const((2, L, gw)), const((gw, gw))]
    return pl.pallas_call(
        functools.partial(_wkv_kernel, n_chunks=rb // L, vres=vres),
        grid=(bn, nq, t // rb),
        in_specs=specs,
        out_specs=act(),
        out_shape=jax.ShapeDtypeStruct((m, d), BF16),
        scratch_shapes=[pltpu.VMEM((gw, gw), F32)],
        compiler_params=_cparams(("parallel", "parallel", "arbitrary"), VMEM_LIMIT),
        name="wkv7",
    )(*ins)


def kernel(x, positions, ret_w_in, ret_gn_g, ret_gn_b, ret_w_o, rwkv_mu, rwkv_w_rkv, rwkv_w0, rwkv_w1, rwkv_w2, rwkv_a0, rwkv_a1, rwkv_a2, rwkv_g1, rwkv_g2, rwkv_k_k, rwkv_k_a, rwkv_r_k, rwkv_gn_g, rwkv_gn_b, rwkv_w_o, rwkv_v0, rwkv_v1, rwkv_v2, ln_mix_g, ln_mix_b, mlp_w1, mlp_w2, ln_mlp_g, ln_mlp_b):
    bn, t, d = x.shape
    m = bn * t
    depth = ln_mix_g.shape[0]
    alpha = (2 * depth) ** 0.25
    assert d % RET_HEAD_DIM == 0 and t % RET_CHUNK == 0 and t % WKV_CHUNK == 0

    xf = x.reshape(m, d).astype(F32)
    xb = xf.astype(BF16)
    cos, sin = _rope_tables(positions)
    v_first = None
    mix_order = jnp.array([0, 2, 3, 1, 4, 5])

    for i in range(depth):
        j = i // N_MIXERS
        if i % N_MIXERS == 0:
            qkvg = _mm2d(xb, ret_w_in[j].astype(BF16))
            gated = _retention_core(qkvg, cos, sin, ret_gn_g[j], ret_gn_b[j], bn, t)
            h = _mm2d(gated, ret_w_o[j].astype(BF16))
        else:
            x6 = _token_mix(xf, rwkv_mu[j][mix_order], t)
            rkv = _matmul(x6, rwkv_w_rkv[j].astype(BF16))
            lw = _lora(x6, 3, rwkv_w1[j], rwkv_w2[j], "tanh", F32)
            la = _lora(x6, 4, rwkv_a1[j], rwkv_a2[j], None, BF16)
            lg = _lora(x6, 5, rwkv_g1[j], rwkv_g2[j], "sigmoid", BF16)
            if j == 0:
                lv, v0 = None, jnp.zeros((d,), F32)
                v_first = rkv
            else:
                lv = _lora(x6, 2, rwkv_v1[j - 1], rwkv_v2[j - 1], None, BF16)
                v0 = rwkv_v0[j - 1]
            params = (rwkv_w0[j], rwkv_a0[j], rwkv_k_k[j], rwkv_k_a[j], rwkv_r_k[j].reshape(d),
                      rwkv_gn_g[j], rwkv_gn_b[j], v0)
            gated = _wkv_core(rkv, lw, la, lg, lv, v_first, params, bn, t)
            h = _mm2d(gated, rwkv_w_o[j].astype(BF16))
        xf, xb = _add_ln(xf, h, ln_mix_g[i], ln_mix_b[i], alpha)
        hid = _mm2d(xb, mlp_w1[i].astype(BF16), act="relu2")
        h = _mm2d(hid, mlp_w2[i].astype(BF16))
        xf, xb = _add_ln(xf, h, ln_mlp_g[i], ln_mlp_b[i], alpha)
    return xf.reshape(bn, t, d).astype(x.dtype)
```

```python
import functools

import jax
import jax.numpy as jnp
from jax import lax
from jax.experimental import pallas as pl
from jax.experimental.pallas import tpu as pltpu

F32 = jnp.float32
BF16 = jnp.bfloat16

RET_HEAD_DIM = 256
RET_CHUNK = 128
ROPE_BASE = 10000.0
RET_GN_EPS = 1e-5
RWKV_HEAD_DIM = 64
RWKV_GN_EPS = 64e-5
LN_EPS = 1e-5
N_MIXERS = 2

LANES = 128
MXU_DIM = 256
WKV_CHUNK = 64
WKV_GROUP = MXU_DIM // RWKV_HEAD_DIM
WKV_INTERLEAVE = 4
RET_INTERLEAVE = 2
VMEM_LIMIT = 56 * 1024 * 1024


def _cparams(sem, vmem=None):
    return pltpu.CompilerParams(dimension_semantics=sem, vmem_limit_bytes=vmem)


def _round_robin(stage_generators):
    live = list(stage_generators)
    while live:
        nxt = []
        for g in live:
            try:
                next(g)
                nxt.append(g)
            except StopIteration:
                pass
        live = nxt


def _act(x, act):
    if act == "relu2":
        return jnp.square(jnp.maximum(x, 0.0))
    return x


def _mm_kernel(x_ref, w_ref, o_ref, *scratch, nk, act):
    part = jnp.dot(x_ref[...], w_ref[...], preferred_element_type=F32)
    if nk == 1:
        o_ref[...] = _act(part, act).astype(o_ref.dtype)
        return
    acc_ref, = scratch
    k = pl.program_id(3)

    @pl.when(k == 0)
    def _():
        acc_ref[...] = part

    @pl.when(k > 0)
    def _():
        acc_ref[...] += part

    @pl.when(k == nk - 1)
    def _():
        o_ref[...] = _act(acc_ref[...], act).astype(o_ref.dtype)


def _matmul(x, w, *, act=None, out_dtype=BF16):
    _, m, kd = x.shape
    g, _, n = w.shape
    tm, tn, tk = min(m, 1024), min(n, 1024), min(kd, 4096)
    nk = kd // tk
    scratch = [pltpu.VMEM((tm, tn), F32)] if nk > 1 else []
    return pl.pallas_call(
        functools.partial(_mm_kernel, nk=nk, act=act),
        grid=(g, m // tm, n // tn, nk),
        in_specs=[pl.BlockSpec((None, tm, tk), lambda b, i, j, k: (b, i, k)),
                  pl.BlockSpec((None, tk, tn), lambda b, i, j, k: (b, k, j))],
        out_specs=pl.BlockSpec((None, tm, tn), lambda b, i, j, k: (b, i, j)),
        out_shape=jax.ShapeDtypeStruct((g, m, n), out_dtype),
        scratch_shapes=scratch,
        compiler_params=_cparams(("parallel", "parallel", "parallel", "arbitrary"), VMEM_LIMIT),
        name="matmul",
    )(x, w)


def _mm2d(x, w, **kw):
    return _matmul(x[None], w[None], **kw)[0]


def _add_ln_kernel(x_ref, h_ref, g_ref, b_ref, of_ref, ob_ref, *, alpha):
    z = alpha * x_ref[...] + h_ref[...].astype(F32)
    mu = jnp.mean(z, axis=-1, keepdims=True)
    zc = z - mu
    var = jnp.mean(zc * zc, axis=-1, keepdims=True)
    y = zc * lax.rsqrt(var + LN_EPS) * g_ref[...] + b_ref[...]
    of_ref[...] = y
    ob_ref[...] = y.astype(BF16)


def _add_ln(x, h, g, b, alpha):
    m, d = x.shape
    tm = min(m, 256)
    row = pl.BlockSpec((tm, d), lambda i: (i, 0))
    vec = pl.BlockSpec((1, d), lambda i: (0, 0))
    return pl.pallas_call(
        functools.partial(_add_ln_kernel, alpha=alpha),
        grid=(m // tm,),
        in_specs=[row, row, vec, vec],
        out_specs=[row, row],
        out_shape=[jax.ShapeDtypeStruct((m, d), F32), jax.ShapeDtypeStruct((m, d), BF16)],
        compiler_params=_cparams(("parallel",), VMEM_LIMIT),
        name="add_ln",
    )(x, h, g.reshape(1, d), b.reshape(1, d))


def _rope_kernel(pos_ref, freq_ref, cos_ref, sin_ref):
    ang = pos_ref[...].astype(F32) * freq_ref[...]
    cos_ref[...] = jnp.cos(ang)
    sin_ref[...] = jnp.sin(ang)


def _rope_tables(positions):
    m = positions.size
    half = RET_HEAD_DIM // 2
    inv_freq = ROPE_BASE ** (-jnp.arange(half, dtype=F32) / half)
    tm = min(m, 512)
    out = pl.BlockSpec((tm, half), lambda i: (i, 0))
    return pl.pallas_call(
        _rope_kernel,
        grid=(m // tm,),
        in_specs=[pl.BlockSpec((tm, 1), lambda i: (i, 0)),
                  pl.BlockSpec((1, half), lambda i: (0, 0))],
        out_specs=[out, out],
        out_shape=[jax.ShapeDtypeStruct((m, half), F32)] * 2,
        compiler_params=_cparams(("parallel",)),
        name="rope_tables",
    )(positions.reshape(m, 1), inv_freq.reshape(1, half))


def _ret_kernel(q_ref, k_ref, v_ref, g_ref, cos_ref, sin_ref, mask_ref, qd_ref, kd_ref, cd_ref,
                gg_ref, gb_ref, o_ref, state_ref, *, n_chunks, n_heads):
    c_len, dh = RET_CHUNK, RET_HEAD_DIM
    half = dh // 2

    @pl.when(pl.program_id(2) == 0)
    def _():
        state_ref[...] = jnp.zeros_like(state_ref)

    def rot(t, cos, sin):
        t1, t2 = t[:, :half], t[:, half:]
        return jnp.concatenate([t1 * cos - t2 * sin, t1 * sin + t2 * cos], axis=-1)

    def head_chunk(rows, hh, cos, sin):
        cols = slice(hh * dh, (hh + 1) * dh)
        q = rot(q_ref[rows, cols].astype(F32), cos, sin)
        k = rot(k_ref[rows, cols].astype(F32), cos, sin) * (dh ** -0.5)
        v = v_ref[rows, cols]
        inner = lax.dot_general(q.astype(BF16), k.astype(BF16), (((1,), (1,)), ((), ())),
                                preferred_element_type=F32)
        st = state_ref[hh]
        cross = jnp.dot((q * qd_ref[hh]).astype(BF16), st.astype(BF16), preferred_element_type=F32)
        kdt = jnp.transpose(k * kd_ref[hh]).astype(BF16)
        upd = jnp.dot(kdt, v, preferred_element_type=F32)
        yield
        state_ref[hh] = st * cd_ref[hh][0:1, :] + upd
        y = jnp.dot((inner * mask_ref[hh]).astype(BF16), v, preferred_element_type=F32) + cross
        yield
        mu = jnp.mean(y, axis=-1, keepdims=True)
        yc = y - mu
        var = jnp.mean(yc * yc, axis=-1, keepdims=True)
        yn = yc * lax.rsqrt(var + RET_GN_EPS) * gg_ref[:, cols] + gb_ref[:, cols]
        g = g_ref[rows, cols].astype(F32)
        o_ref[rows, cols] = (g * jax.nn.sigmoid(g) * yn).astype(o_ref.dtype)

    def body(c, carry):
        rows = pl.ds(pl.multiple_of(c * c_len, c_len), c_len)
        cos, sin = cos_ref[rows, :], sin_ref[rows, :]
        _round_robin([head_chunk(rows, hh, cos, sin) for hh in range(n_heads)])
        return carry

    lax.fori_loop(0, n_chunks, body, 0)


def _retention_core(qkvg, cos, sin, gn_g, gn_b, bn, t):
    d = qkvg.shape[1] // 4
    h = d // RET_HEAD_DIM
    c_len, dh = RET_CHUNK, RET_HEAD_DIM
    log_g = jnp.log1p(-(2.0 ** (-5.0 - jnp.arange(h, dtype=F32))))
    idx = jnp.arange(c_len, dtype=F32)
    diff = idx[:, None] - idx[None, :]
    mask = jnp.where(diff >= 0, jnp.exp(log_g[:, None, None] * jnp.maximum(diff, 0.0)), 0.0)
    q_decay = jnp.exp(log_g[:, None] * (idx + 1.0))
    k_decay = jnp.exp(log_g[:, None] * (c_len - 1.0 - idx))
    chunk_decay = jnp.exp(log_g * c_len)
    qd = jnp.broadcast_to(q_decay[:, :, None], (h, c_len, dh))
    kd = jnp.broadcast_to(k_decay[:, :, None], (h, c_len, dh))
    cd = jnp.broadcast_to(chunk_decay[:, None, None], (h, 8, dh))

    nh = min(RET_INTERLEAVE, h)
    bw = nh * dh
    rb = min(t, 1024)
    nr = t // rb
    hb = h // nh

    def col(off):
        return pl.BlockSpec((rb, bw), lambda b, j, i, off=off: (b * nr + i, off * hb + j))

    def per_head(rows, width):
        return pl.BlockSpec((nh, rows, width), lambda b, j, i: (j, 0, 0))

    tab = pl.BlockSpec((rb, dh // 2), lambda b, j, i: (b * nr + i, 0))
    vec = pl.BlockSpec((1, bw), lambda b, j, i: (0, j))
    return pl.pallas_call(
        functools.partial(_ret_kernel, n_chunks=rb // c_len, n_heads=nh),
        grid=(bn, hb, nr),
        in_specs=[col(0), col(1), col(2), col(3), tab, tab,
                  per_head(c_len, c_len), per_head(c_len, dh), per_head(c_len, dh), per_head(8, dh),
                  vec, vec],
        out_specs=pl.BlockSpec((rb, bw), lambda b, j, i: (b * nr + i, j)),
        out_shape=jax.ShapeDtypeStruct((bn * t, d), BF16),
        scratch_shapes=[pltpu.VMEM((nh, dh, dh), F32)],
        compiler_params=_cparams(("parallel", "parallel", "arbitrary"), VMEM_LIMIT),
        name="retention",
    )(qkvg, qkvg, qkvg, qkvg, cos, sin, mask, qd, kd, cd, gn_g.reshape(1, d), gn_b.reshape(1, d))


def _mix_kernel(x_ref, prev_ref, mu_ref, o_ref, *, tiles_per_seq):
    x = x_ref[...]
    first = (pl.program_id(0) % tiles_per_seq) == 0
    prev_row = jnp.where(first, 0.0, prev_ref[7:8, :])
    shifted = pltpu.roll(x, 1, axis=0)
    rid = lax.broadcasted_iota(jnp.int32, x.shape, 0)
    xx = jnp.where(rid == 0, prev_row, shifted) - x
    for i in range(o_ref.shape[0]):
        o_ref[i] = (x + xx * mu_ref[i:i + 1, :]).astype(o_ref.dtype)


def _token_mix(x, mu, t):
    m, d = x.shape
    tm = min(t, 256)
    return pl.pallas_call(
        functools.partial(_mix_kernel, tiles_per_seq=t // tm),
        grid=(m // tm,),
        in_specs=[pl.BlockSpec((tm, d), lambda i: (i, 0)),
                  pl.BlockSpec((8, d), lambda i: (jnp.maximum(i * (tm // 8) - 1, 0), 0)),
                  pl.BlockSpec((6, d), lambda i: (0, 0))],
        out_specs=pl.BlockSpec((6, tm, d), lambda i: (0, i, 0)),
        out_shape=jax.ShapeDtypeStruct((6, m, d), BF16),
        compiler_params=_cparams(("parallel",), VMEM_LIMIT),
        name="token_mix",
    )(x, x, mu)


def _lora_kernel(x_ref, w1_ref, w2_ref, o_ref, *, act):
    t = jnp.dot(x_ref[...], w1_ref[...], preferred_element_type=F32)
    if act == "tanh":
        t = jnp.tanh(t)
    elif act == "sigmoid":
        t = jax.nn.sigmoid(t)
    o_ref[...] = jnp.dot(t.astype(BF16), w2_ref[...], preferred_element_type=F32).astype(o_ref.dtype)


def _lora(x6, idx, w1, w2, act, out_dtype):
    _, m, d = x6.shape
    r = w1.shape[1]
    rp = -(-r // LANES) * LANES
    w1p = jnp.pad(w1, ((0, 0), (0, rp - r))).astype(BF16)
    w2p = jnp.pad(w2, ((0, rp - r), (0, 0))).astype(BF16)
    tm = min(m, 512)
    return pl.pallas_call(
        functools.partial(_lora_kernel, act=act),
        grid=(m // tm,),
        in_specs=[pl.BlockSpec((None, tm, d), lambda i: (idx, i, 0)),
                  pl.BlockSpec((d, rp), lambda i: (0, 0)),
                  pl.BlockSpec((rp, d), lambda i: (0, 0))],
        out_specs=pl.BlockSpec((tm, d), lambda i: (i, 0)),
        out_shape=jax.ShapeDtypeStruct((m, d), out_dtype),
        compiler_params=_cparams(("parallel",), VMEM_LIMIT),
        name="lora",
    )(x6, w1p, w2p)


def _split2(x):
    hi = x.astype(BF16)
    return hi, (x - hi.astype(F32)).astype(BF16)


def _block_diag(y, half_masks):
    yb = y.astype(BF16)
    zeros = jnp.zeros((y.shape[0], LANES), BF16)
    blocks = []
    for h in range(WKV_GROUP):
        col = h // 2
        part = yb[:, col * LANES:(col + 1) * LANES] * half_masks[h % 2]
        blocks.append(jnp.concatenate([part, zeros] if col == 0 else [zeros, part], axis=1))
    return jnp.concatenate(blocks, axis=0)


def _wkv_kernel(r_ref, k_ref, v_ref, lw_ref, la_ref, lg_ref, *rest, n_chunks, n_groups, vres):
    if vres:
        lv_ref, vf_ref = rest[0], rest[1]
        rest = rest[2:]
    (w0_ref, a0_ref, kk_ref, ka_ref, rk_ref, gg_ref, gb_ref, v0_ref,
     ones_ref, tri_ref, lo_ref, bdm_ref, o_ref, state_ref) = rest
    L, gw = WKV_CHUNK, MXU_DIM

    @pl.when(pl.program_id(2) == 0)
    def _():
        state_ref[...] = jnp.zeros_like(state_ref)

    lane = lax.broadcasted_iota(jnp.int32, (1, LANES), 1)
    half_masks = [(lane < RWKV_HEAD_DIM).astype(BF16), (lane >= RWKV_HEAD_DIM).astype(BF16)]

    def mm(a, b):
        return jnp.dot(a, b, preferred_element_type=F32)

    def seg_sum(x):
        return mm(x.astype(BF16), ones_ref[...])

    def bd(y):
        return _block_diag(y, half_masks)

    def group_chunk(rows, q):
        cols = slice(q * gw, (q + 1) * gw)
        r = r_ref[rows, cols].astype(F32)
        k0 = k_ref[rows, cols].astype(F32)
        v = v_ref[rows, cols].astype(F32)
        z = -(w0_ref[:, cols] + lw_ref[rows, cols])
        softplus = jnp.maximum(z, 0.0) + jnp.log1p(jnp.exp(-jnp.abs(z)))
        logw = -jnp.exp(-softplus - 0.5)
        a = jax.nn.sigmoid(a0_ref[:, cols] + la_ref[rows, cols].astype(F32))
        if vres:
            v = v + (vf_ref[rows, cols].astype(F32) - v) * jax.nn.sigmoid(
                v0_ref[:, cols] + lv_ref[rows, cols].astype(F32))
        kk = k0 * kk_ref[:, cols]
        ss = seg_sum(kk * kk)
        hi, lo = _split2(logw)
        cum = mm(tri_ref[...], hi) + mm(tri_ref[...], lo)
        yield
        kk = kk * lax.rsqrt(jnp.maximum(ss, 1e-24))
        k = k0 * (1.0 + (a - 1.0) * ka_ref[:, cols])
        avec, bvec = -kk, kk * a
        cum_l = cum[L - 1:L, :]
        e_out = jnp.exp(-cum)
        tail = jnp.exp(cum_l - cum)
        at = (avec * jnp.exp(cum - logw)).astype(BF16)
        rt = (r * jnp.exp(cum)).astype(BF16)
        ar = jnp.concatenate([at, rt], axis=0)
        bk = jnp.concatenate([bd(bvec * e_out), bd(k * e_out)], axis=0)
        p = lax.dot_general(ar, bk, (((1,), (1,)), ((), ())), preferred_element_type=F32)
        st = state_ref[q]
        ph = lax.dot_general(ar, st.astype(BF16), (((1,), (1,)), ((), ())),
                             preferred_element_type=F32)
        bonus_s = seg_sum(r * k * rk_ref[:, cols])
        yield
        strict, incl = lo_ref[0], lo_ref[1]
        m_ab = p[:L, :gw] * strict
        m_ak = p[:L, gw:] * strict
        n_ab = p[L:, :gw] * incl
        n_ak = p[L:, gw:] * incl
        pv = mm(jnp.concatenate([m_ak, n_ak], axis=0).astype(BF16), bd(v))
        x = m_ab
        x2 = mm(x.astype(BF16), bd(x))
        yield
        u = ph[:L] + pv[:L]
        for j in range(6):
            du = mm(x.astype(BF16), bd(u))
            if j < 5:
                x = x2
            if j < 4:
                x2 = mm(x.astype(BF16), bd(x))
            yield
            u = u + du

        yn_u = mm(n_ab.astype(BF16), bd(u))
        uv = jnp.concatenate([u, v], axis=0).astype(BF16)
        bkh = jnp.concatenate([bvec * tail, k * tail], axis=0).astype(BF16)
        upd = lax.dot_general(uv, bkh, (((0,), (0,)), ((), ())), preferred_element_type=F32)
        yield
        state_ref[q] = st * jnp.exp(cum_l) + upd * bdm_ref[...]
        y = ph[L:] + pv[L:] + yn_u
        mu = seg_sum(y) * (1.0 / RWKV_HEAD_DIM)
        yield
        yc = y - mu
        var = seg_sum(yc * yc) * (1.0 / RWKV_HEAD_DIM)
        yield
        yn = yc * lax.rsqrt(var + RWKV_GN_EPS) * gg_ref[:, cols] + gb_ref[:, cols]
        o_ref[rows, cols] = ((yn + bonus_s * v) * lg_ref[rows, cols].astype(F32)).astype(o_ref.dtype)

    def body(c, carry):
        rows = pl.ds(pl.multiple_of(c * L, L), L)
        _round_robin([group_chunk(rows, q) for q in range(n_groups)])
        return carry

    lax.fori_loop(0, n_chunks, body, 0)


def _wkv_core(rkv, lw, la, lg, lv, v_first, params, bn, t):
    _, m, d = rkv.shape
    L, gw = WKV_CHUNK, MXU_DIM
    n_groups = min(WKV_INTERLEAVE, d // gw)
    bw = n_groups * gw
    rb = min(t, 512)
    vres = lv is not None
    hid = jnp.arange(gw) // RWKV_HEAD_DIM
    bdm = (hid[:, None] == hid[None, :])
    ones_bd = bdm.astype(BF16)
    ti = jnp.arange(L)
    tri = (ti[:, None] >= ti[None, :]).astype(BF16)
    si = jnp.arange(gw) % L
    lo_masks = jnp.stack([(si[None, :] < ti[:, None]), (si[None, :] <= ti[:, None])]).astype(F32)

    def act(g=None):
        if g is None:
            return pl.BlockSpec((rb, bw), lambda b, q, i: (b * (t // rb) + i, q))
        return pl.BlockSpec((None, rb, bw), lambda b, q, i, g=g: (g, b * (t // rb) + i, q))

    vec = pl.BlockSpec((1, bw), lambda b, q, i: (0, q))

    def const(shape):
        return pl.BlockSpec(shape, lambda b, q, i: (0,) * len(shape))

    ins = [rkv, rkv, rkv, lw, la, lg]
    specs = [act(0), act(1), act(2), act(), act(), act()]
    if vres:
        ins += [lv, v_first]
        specs += [act(), act(2)]
    ins += [p.reshape(1, d) for p in params]
    specs += [vec] * len(params)
    ins += [ones_bd, tri, lo_masks, bdm.astype(F32)]
    specs += [const((gw, gw)), const((L, L)), const((2, L, gw)), const((gw, gw))]
    return pl.pallas_call(
        functools.partial(_wkv_kernel, n_chunks=rb // L, n_groups=n_groups, vres=vres),
        grid=(bn, d // bw, t // rb),
        in_specs=specs,
        out_specs=act(),
        out_shape=jax.ShapeDtypeStruct((m, d), BF16),
        scratch_shapes=[pltpu.VMEM((n_groups, gw, gw), F32)],
        compiler_params=_cparams(("parallel", "parallel", "arbitrary"), VMEM_LIMIT),
        name="wkv7",
    )(*ins)


def kernel(x, positions, ret_w_in, ret_gn_g, ret_gn_b, ret_w_o, rwkv_mu, rwkv_w_rkv, rwkv_w0, rwkv_w1, rwkv_w2, rwkv_a0, rwkv_a1, rwkv_a2, rwkv_g1, rwkv_g2, rwkv_k_k, rwkv_k_a, rwkv_r_k, rwkv_gn_g, rwkv_gn_b, rwkv_w_o, rwkv_v0, rwkv_v1, rwkv_v2, ln_mix_g, ln_mix_b, mlp_w1, mlp_w2, ln_mlp_g, ln_mlp_b):
    bn, t, d = x.shape
    m = bn * t
    depth = ln_mix_g.shape[0]
    alpha = (2 * depth) ** 0.25
    assert d % RET_HEAD_DIM == 0 and t % RET_CHUNK == 0 and t % WKV_CHUNK == 0

    xf = x.reshape(m, d).astype(F32)
    xb = xf.astype(BF16)
    cos, sin = _rope_tables(positions)
    v_first = None
    mix_order = jnp.array([0, 2, 3, 1, 4, 5])

    for i in range(depth):
        j = i // N_MIXERS
        if i % N_MIXERS == 0:
            qkvg = _mm2d(xb, ret_w_in[j].astype(BF16))
            gated = _retention_core(qkvg, cos, sin, ret_gn_g[j], ret_gn_b[j], bn, t)
            h = _mm2d(gated, ret_w_o[j].astype(BF16))
        else:
            x6 = _token_mix(xf, rwkv_mu[j][mix_order], t)
            rkv = _matmul(x6, rwkv_w_rkv[j].astype(BF16))
            lw = _lora(x6, 3, rwkv_w1[j], rwkv_w2[j], "tanh", F32)
            la = _lora(x6, 4, rwkv_a1[j], rwkv_a2[j], None, BF16)
            lg = _lora(x6, 5, rwkv_g1[j], rwkv_g2[j], "sigmoid", BF16)
            if j == 0:
                lv, v0 = None, jnp.zeros((d,), F32)
                v_first = rkv
            else:
                lv = _lora(x6, 2, rwkv_v1[j - 1], rwkv_v2[j - 1], None, BF16)
                v0 = rwkv_v0[j - 1]
            params = (rwkv_w0[j], rwkv_a0[j], rwkv_k_k[j], rwkv_k_a[j], rwkv_r_k[j].reshape(d),
                      rwkv_gn_g[j], rwkv_gn_b[j], v0)
            gated = _wkv_core(rkv, lw, la, lg, lv, v_first, params, bn, t)
            h = _mm2d(gated, rwkv_w_o[j].astype(BF16))
        xf, xb = _add_ln(xf, h, ln_mix_g[i], ln_mix_b[i], alpha)
        hid = _mm2d(xb, mlp_w1[i].astype(BF16), act="relu2")
        h = _mm2d(hid, mlp_w2[i].astype(BF16))
        xf, xb = _add_ln(xf, h, ln_mlp_g[i], ln_mlp_b[i], alpha)
    return xf.reshape(bn, t, d).astype(x.dtype)
```

```python
import functools

import jax
import jax.numpy as jnp
from jax import lax
from jax.experimental import pallas as pl
from jax.experimental.pallas import tpu as pltpu

F32 = jnp.float32
BF16 = jnp.bfloat16

RET_HEAD_DIM = 256
RET_CHUNK = 128
ROPE_BASE = 10000.0
RET_GN_EPS = 1e-5
RWKV_HEAD_DIM = 64
RWKV_GN_EPS = 64e-5
LN_EPS = 1e-5
N_MIXERS = 2

LANES = 128
MXU_DIM = 256
WKV_CHUNK = 64
WKV_GROUP = MXU_DIM // RWKV_HEAD_DIM
WKV_INTERLEAVE = 8
RET_INTERLEAVE = 4
VMEM_LIMIT = 56 * 1024 * 1024


def _cparams(sem, vmem=None):
    return pltpu.CompilerParams(dimension_semantics=sem, vmem_limit_bytes=vmem)


def _round_robin(stage_generators):
    live = list(stage_generators)
    while live:
        nxt = []
        for g in live:
            try:
                next(g)
                nxt.append(g)
            except StopIteration:
                pass
        live = nxt


def _act(x, act):
    if act == "relu2":
        return jnp.square(jnp.maximum(x, 0.0))
    return x


def _mm_kernel(x_ref, w_ref, o_ref, *scratch, nk, act):
    part = jnp.dot(x_ref[...], w_ref[...].astype(BF16), preferred_element_type=F32)
    if nk == 1:
        o_ref[...] = _act(part, act).astype(o_ref.dtype)
        return
    acc_ref, = scratch
    k = pl.program_id(3)

    @pl.when(k == 0)
    def _():
        acc_ref[...] = part

    @pl.when(k > 0)
    def _():
        acc_ref[...] += part

    @pl.when(k == nk - 1)
    def _():
        o_ref[...] = _act(acc_ref[...], act).astype(o_ref.dtype)


def _matmul(x, w, *, g=1, w_off=0, act=None, out_dtype=BF16):
    _, m, kd = x.shape
    _, _, n = w.shape
    tm, tn, tk = min(m, 1024), min(n, 512), min(kd, 4096)
    nk = kd // tk
    scratch = [pltpu.VMEM((tm, tn), F32)] if nk > 1 else []
    return pl.pallas_call(
        functools.partial(_mm_kernel, nk=nk, act=act),
        grid=(g, m // tm, n // tn, nk),
        in_specs=[pl.BlockSpec((None, tm, tk), lambda b, i, j, k: (b, i, k)),
                  pl.BlockSpec((None, tk, tn), lambda b, i, j, k: (w_off + b, k, j))],
        out_specs=pl.BlockSpec((None, tm, tn), lambda b, i, j, k: (b, i, j)),
        out_shape=jax.ShapeDtypeStruct((g, m, n), out_dtype),
        scratch_shapes=scratch,
        compiler_params=_cparams(("parallel", "parallel", "parallel", "arbitrary"), VMEM_LIMIT),
        name="matmul",
    )(x, w)


def _mm2d(x, w_stack, layer, **kw):
    return _matmul(x[None], w_stack, w_off=layer, **kw)[0]


def _add_ln_kernel(x_ref, h_ref, g_ref, b_ref, of_ref, ob_ref, *, alpha):
    z = alpha * x_ref[...] + h_ref[...].astype(F32)
    mu = jnp.mean(z, axis=-1, keepdims=True)
    zc = z - mu
    var = jnp.mean(zc * zc, axis=-1, keepdims=True)
    y = zc * lax.rsqrt(var + LN_EPS) * g_ref[...] + b_ref[...]
    of_ref[...] = y
    ob_ref[...] = y.astype(BF16)


def _add_ln(x, h, g, b, alpha):
    m, d = x.shape
    tm = min(m, 256)
    row = pl.BlockSpec((tm, d), lambda i: (i, 0))
    vec = pl.BlockSpec((1, d), lambda i: (0, 0))
    return pl.pallas_call(
        functools.partial(_add_ln_kernel, alpha=alpha),
        grid=(m // tm,),
        in_specs=[row, row, vec, vec],
        out_specs=[row, row],
        out_shape=[jax.ShapeDtypeStruct((m, d), F32), jax.ShapeDtypeStruct((m, d), BF16)],
        compiler_params=_cparams(("parallel",), VMEM_LIMIT),
        name="add_ln",
    )(x, h, g.reshape(1, d), b.reshape(1, d))


def _rope_kernel(pos_ref, freq_ref, cos_ref, sin_ref):
    ang = pos_ref[...].astype(F32) * freq_ref[...]
    cos_ref[...] = jnp.cos(ang)
    sin_ref[...] = jnp.sin(ang)


def _rope_tables(positions):
    m = positions.size
    half = RET_HEAD_DIM // 2
    inv_freq = ROPE_BASE ** (-jnp.arange(half, dtype=F32) / half)
    tm = min(m, 512)
    out = pl.BlockSpec((tm, half), lambda i: (i, 0))
    return pl.pallas_call(
        _rope_kernel,
        grid=(m // tm,),
        in_specs=[pl.BlockSpec((tm, 1), lambda i: (i, 0)),
                  pl.BlockSpec((1, half), lambda i: (0, 0))],
        out_specs=[out, out],
        out_shape=[jax.ShapeDtypeStruct((m, half), F32)] * 2,
        compiler_params=_cparams(("parallel",)),
        name="rope_tables",
    )(positions.reshape(m, 1), inv_freq.reshape(1, half))


def _ret_kernel(q_ref, k_ref, v_ref, g_ref, cos_ref, sin_ref, mask_ref, qd_ref, kd_ref, cd_ref,
                gg_ref, gb_ref, o_ref, state_ref, *, n_chunks, n_heads):
    c_len, dh = RET_CHUNK, RET_HEAD_DIM
    half = dh // 2

    @pl.when(pl.program_id(2) == 0)
    def _():
        state_ref[...] = jnp.zeros_like(state_ref)

    def rot(t, cos, sin):
        t1, t2 = t[:, :half], t[:, half:]
        return jnp.concatenate([t1 * cos - t2 * sin, t1 * sin + t2 * cos], axis=-1)

    def head_chunk(rows, hh, cos, sin):
        cols = slice(hh * dh, (hh + 1) * dh)
        q = rot(q_ref[rows, cols].astype(F32), cos, sin)
        k = rot(k_ref[rows, cols].astype(F32), cos, sin) * (dh ** -0.5)
        v = v_ref[rows, cols]
        inner = lax.dot_general(q.astype(BF16), k.astype(BF16), (((1,), (1,)), ((), ())),
                                preferred_element_type=F32)
        st = state_ref[hh]
        cross = jnp.dot((q * qd_ref[hh]).astype(BF16), st.astype(BF16), preferred_element_type=F32)
        kdt = jnp.transpose(k * kd_ref[hh]).astype(BF16)
        upd = jnp.dot(kdt, v, preferred_element_type=F32)
        yield
        state_ref[hh] = st * cd_ref[hh][0:1, :] + upd
        y = jnp.dot((inner * mask_ref[hh]).astype(BF16), v, preferred_element_type=F32) + cross
        yield
        mu = jnp.mean(y, axis=-1, keepdims=True)
        yc = y - mu
        var = jnp.mean(yc * yc, axis=-1, keepdims=True)
        yn = yc * lax.rsqrt(var + RET_GN_EPS) * gg_ref[:, cols] + gb_ref[:, cols]
        g = g_ref[rows, cols].astype(F32)
        o_ref[rows, cols] = (g * jax.nn.sigmoid(g) * yn).astype(o_ref.dtype)

    def body(c, carry):
        rows = pl.ds(pl.multiple_of(c * c_len, c_len), c_len)
        cos, sin = cos_ref[rows, :], sin_ref[rows, :]
        _round_robin([head_chunk(rows, hh, cos, sin) for hh in range(n_heads)])
        return carry

    lax.fori_loop(0, n_chunks, body, 0)


def _retention_core(qkvg, cos, sin, gn_g, gn_b, bn, t):
    d = qkvg.shape[1] // 4
    h = d // RET_HEAD_DIM
    c_len, dh = RET_CHUNK, RET_HEAD_DIM
    log_g = jnp.log1p(-(2.0 ** (-5.0 - jnp.arange(h, dtype=F32))))
    idx = jnp.arange(c_len, dtype=F32)
    diff = idx[:, None] - idx[None, :]
    mask = jnp.where(diff >= 0, jnp.exp(log_g[:, None, None] * jnp.maximum(diff, 0.0)), 0.0)
    q_decay = jnp.exp(log_g[:, None] * (idx + 1.0))
    k_decay = jnp.exp(log_g[:, None] * (c_len - 1.0 - idx))
    chunk_decay = jnp.exp(log_g * c_len)
    qd = jnp.broadcast_to(q_decay[:, :, None], (h, c_len, dh))
    kd = jnp.broadcast_to(k_decay[:, :, None], (h, c_len, dh))
    cd = jnp.broadcast_to(chunk_decay[:, None, None], (h, 8, dh))

    nh = min(RET_INTERLEAVE, h)
    bw = nh * dh
    rb = min(t, 1024)
    nr = t // rb
    hb = h // nh

    def col(off):
        return pl.BlockSpec((rb, bw), lambda b, j, i, off=off: (b * nr + i, off * hb + j))

    def per_head(rows, width):
        return pl.BlockSpec((nh, rows, width), lambda b, j, i: (j, 0, 0))

    tab = pl.BlockSpec((rb, dh // 2), lambda b, j, i: (b * nr + i, 0))
    vec = pl.BlockSpec((1, bw), lambda b, j, i: (0, j))
    return pl.pallas_call(
        functools.partial(_ret_kernel, n_chunks=rb // c_len, n_heads=nh),
        grid=(bn, hb, nr),
        in_specs=[col(0), col(1), col(2), col(3), tab, tab,
                  per_head(c_len, c_len), per_head(c_len, dh), per_head(c_len, dh), per_head(8, dh),
                  vec, vec],
        out_specs=pl.BlockSpec((rb, bw), lambda b, j, i: (b * nr + i, j)),
        out_shape=jax.ShapeDtypeStruct((bn * t, d), BF16),
        scratch_shapes=[pltpu.VMEM((nh, dh, dh), F32)],
        compiler_params=_cparams(("parallel", "parallel", "arbitrary"), VMEM_LIMIT),
        name="retention",
    )(qkvg, qkvg, qkvg, qkvg, cos, sin, mask, qd, kd, cd, gn_g.reshape(1, d), gn_b.reshape(1, d))


def _mix_kernel(x_ref, prev_ref, mu_ref, o_ref, *, tiles_per_seq):
    x = x_ref[...]
    first = (pl.program_id(0) % tiles_per_seq) == 0
    prev_row = jnp.where(first, 0.0, prev_ref[7:8, :])
    shifted = pltpu.roll(x, 1, axis=0)
    rid = lax.broadcasted_iota(jnp.int32, x.shape, 0)
    xx = jnp.where(rid == 0, prev_row, shifted) - x
    for i in range(o_ref.shape[0]):
        o_ref[i] = (x + xx * mu_ref[i:i + 1, :]).astype(o_ref.dtype)


def _token_mix(x, mu, t):
    m, d = x.shape
    tm = min(t, 256)
    return pl.pallas_call(
        functools.partial(_mix_kernel, tiles_per_seq=t // tm),
        grid=(m // tm,),
        in_specs=[pl.BlockSpec((tm, d), lambda i: (i, 0)),
                  pl.BlockSpec((8, d), lambda i: (jnp.maximum(i * (tm // 8) - 1, 0), 0)),
                  pl.BlockSpec((6, d), lambda i: (0, 0))],
        out_specs=pl.BlockSpec((6, tm, d), lambda i: (0, i, 0)),
        out_shape=jax.ShapeDtypeStruct((6, m, d), BF16),
        compiler_params=_cparams(("parallel",), VMEM_LIMIT),
        name="token_mix",
    )(x, x, mu)


def _lora_kernel(x_ref, w1_ref, w2_ref, o_ref, *, act):
    t = jnp.dot(x_ref[...], w1_ref[...], preferred_element_type=F32)
    if act == "tanh":
        t = jnp.tanh(t)
    elif act == "sigmoid":
        t = jax.nn.sigmoid(t)
    o_ref[...] = jnp.dot(t.astype(BF16), w2_ref[...], preferred_element_type=F32).astype(o_ref.dtype)


def _lora(x6, idx, w1, w2, act, out_dtype):
    _, m, d = x6.shape
    r = w1.shape[1]
    rp = -(-r // LANES) * LANES
    w1p = jnp.pad(w1, ((0, 0), (0, rp - r))).astype(BF16)
    w2p = jnp.pad(w2, ((0, rp - r), (0, 0))).astype(BF16)
    tm = min(m, 512)
    return pl.pallas_call(
        functools.partial(_lora_kernel, act=act),
        grid=(m // tm,),
        in_specs=[pl.BlockSpec((None, tm, d), lambda i: (idx, i, 0)),
                  pl.BlockSpec((d, rp), lambda i: (0, 0)),
                  pl.BlockSpec((rp, d), lambda i: (0, 0))],
        out_specs=pl.BlockSpec((tm, d), lambda i: (i, 0)),
        out_shape=jax.ShapeDtypeStruct((m, d), out_dtype),
        compiler_params=_cparams(("parallel",), VMEM_LIMIT),
        name="lora",
    )(x6, w1p, w2p)


def _split2(x):
    hi = x.astype(BF16)
    return hi, (x - hi.astype(F32)).astype(BF16)


def _block_diag(y, half_masks):
    yb = y.astype(BF16)
    zeros = jnp.zeros((y.shape[0], LANES), BF16)
    blocks = []
    for h in range(WKV_GROUP):
        col = h // 2
        part = yb[:, col * LANES:(col + 1) * LANES] * half_masks[h % 2]
        blocks.append(jnp.concatenate([part, zeros] if col == 0 else [zeros, part], axis=1))
    return jnp.concatenate(blocks, axis=0)


def _wkv_kernel(r_ref, k_ref, v_ref, lw_ref, la_ref, lg_ref, *rest, n_chunks, n_groups, vres):
    if vres:
        lv_ref, vf_ref = rest[0], rest[1]
        rest = rest[2:]
    (w0_ref, a0_ref, kk_ref, ka_ref, rk_ref, gg_ref, gb_ref, v0_ref,
     ones_ref, tri_ref, lo_ref, bdm_ref, o_ref, state_ref) = rest
    L, gw = WKV_CHUNK, MXU_DIM

    @pl.when(pl.program_id(2) == 0)
    def _():
        state_ref[...] = jnp.zeros_like(state_ref)

    lane = lax.broadcasted_iota(jnp.int32, (1, LANES), 1)
    half_masks = [(lane < RWKV_HEAD_DIM).astype(BF16), (lane >= RWKV_HEAD_DIM).astype(BF16)]

    def mm(a, b):
        return jnp.dot(a, b, preferred_element_type=F32)

    def seg_sum(x):
        return mm(x.astype(BF16), ones_ref[...])

    def bd(y):
        return _block_diag(y, half_masks)

    def group_chunk(rows, q):
        cols = slice(q * gw, (q + 1) * gw)
        r = r_ref[rows, cols].astype(F32)
        k0 = k_ref[rows, cols].astype(F32)
        v = v_ref[rows, cols].astype(F32)
        z = -(w0_ref[:, cols] + lw_ref[rows, cols])
        softplus = jnp.maximum(z, 0.0) + jnp.log1p(jnp.exp(-jnp.abs(z)))
        logw = -jnp.exp(-softplus - 0.5)
        a = jax.nn.sigmoid(a0_ref[:, cols] + la_ref[rows, cols].astype(F32))
        if vres:
            v = v + (vf_ref[rows, cols].astype(F32) - v) * jax.nn.sigmoid(
                v0_ref[:, cols] + lv_ref[rows, cols].astype(F32))
        kk = k0 * kk_ref[:, cols]
        ss = seg_sum(kk * kk)
        hi, lo = _split2(logw)
        cum = mm(tri_ref[...], hi) + mm(tri_ref[...], lo)
        yield
        kk = kk * lax.rsqrt(jnp.maximum(ss, 1e-24))
        k = k0 * (1.0 + (a - 1.0) * ka_ref[:, cols])
        avec, bvec = -kk, kk * a
        cum_l = cum[L - 1:L, :]
        e_out = jnp.exp(-cum)
        tail = jnp.exp(cum_l - cum)
        at = (avec * jnp.exp(cum - logw)).astype(BF16)
        rt = (r * jnp.exp(cum)).astype(BF16)
        ar = jnp.concatenate([at, rt], axis=0)
        bk = jnp.concatenate([bd(bvec * e_out), bd(k * e_out)], axis=0)
        p = lax.dot_general(ar, bk, (((1,), (1,)), ((), ())), preferred_element_type=F32)
        st = state_ref[q]
        ph = lax.dot_general(ar, st.astype(BF16), (((1,), (1,)), ((), ())),
                             preferred_element_type=F32)
        bonus_s = seg_sum(r * k * rk_ref[:, cols])
        yield
        strict, incl = lo_ref[0], lo_ref[1]
        m_ab = p[:L, :gw] * strict
        m_ak = p[:L, gw:] * strict
        n_ab = p[L:, :gw] * incl
        n_ak = p[L:, gw:] * incl
        pv = mm(jnp.concatenate([m_ak, n_ak], axis=0).astype(BF16), bd(v))
        x = m_ab
        x2 = mm(x.astype(BF16), bd(x))
        yield
        u = ph[:L] + pv[:L]
        for j in range(6):
            du = mm(x.astype(BF16), bd(u))
            if j < 5:
                x = x2
            if j < 4:
                x2 = mm(x.astype(BF16), bd(x))
            yield
            u = u + du

        yn_u = mm(n_ab.astype(BF16), bd(u))
        uv = jnp.concatenate([u, v], axis=0).astype(BF16)
        bkh = jnp.concatenate([bvec * tail, k * tail], axis=0).astype(BF16)
        upd = lax.dot_general(uv, bkh, (((0,), (0,)), ((), ())), preferred_element_type=F32)
        yield
        state_ref[q] = st * jnp.exp(cum_l) + upd * bdm_ref[...]
        y = ph[L:] + pv[L:] + yn_u
        mu = seg_sum(y) * (1.0 / RWKV_HEAD_DIM)
        yield
        yc = y - mu
        var = seg_sum(yc * yc) * (1.0 / RWKV_HEAD_DIM)
        yield
        yn = yc * lax.rsqrt(var + RWKV_GN_EPS) * gg_ref[:, cols] + gb_ref[:, cols]
        o_ref[rows, cols] = ((yn + bonus_s * v) * lg_ref[rows, cols].astype(F32)).astype(o_ref.dtype)

    def body(c, carry):
        rows = pl.ds(pl.multiple_of(c * L, L), L)
        _round_robin([group_chunk(rows, q) for q in range(n_groups)])
        return carry

    lax.fori_loop(0, n_chunks, body, 0)


def _wkv_core(rkv, lw, la, lg, lv, v_first, params, bn, t):
    _, m, d = rkv.shape
    L, gw = WKV_CHUNK, MXU_DIM
    n_groups = min(WKV_INTERLEAVE, d // gw)
    bw = n_groups * gw
    rb = min(t, 256)
    vres = lv is not None
    hid = jnp.arange(gw) // RWKV_HEAD_DIM
    bdm = (hid[:, None] == hid[None, :])
    ones_bd = bdm.astype(BF16)
    ti = jnp.arange(L)
    tri = (ti[:, None] >= ti[None, :]).astype(BF16)
    si = jnp.arange(gw) % L
    lo_masks = jnp.stack([(si[None, :] < ti[:, None]), (si[None, :] <= ti[:, None])]).astype(F32)

    def act(g=None):
        if g is None:
            return pl.BlockSpec((rb, bw), lambda b, q, i: (b * (t // rb) + i, q))
        return pl.BlockSpec((None, rb, bw), lambda b, q, i, g=g: (g, b * (t // rb) + i, q))

    vec = pl.BlockSpec((1, bw), lambda b, q, i: (0, q))

    def const(shape):
        return pl.BlockSpec(shape, lambda b, q, i: (0,) * len(shape))

    ins = [rkv, rkv, rkv, lw, la, lg]
    specs = [act(0), act(1), act(2), act(), act(), act()]
    if vres:
        ins += [lv, v_first]
        specs += [act(), act(2)]
    ins += [p.reshape(1, d) for p in params]
    specs += [vec] * len(params)
    ins += [ones_bd, tri, lo_masks, bdm.astype(F32)]
    specs += [const((gw, gw)), const((L, L)), const((2, L, gw)), const((gw, gw))]
    return pl.pallas_call(
        functools.partial(_wkv_kernel, n_chunks=rb // L, n_groups=n_groups, vres=vres),
        grid=(bn, d // bw, t // rb),
        in_specs=specs,
        out_specs=act(),
        out_shape=jax.ShapeDtypeStruct((m, d), BF16),
        scratch_shapes=[pltpu.VMEM((n_groups, gw, gw), F32)],
        compiler_params=_cparams(("parallel", "parallel", "arbitrary"), VMEM_LIMIT),
        name="wkv7",
    )(*ins)


def kernel(x, positions, ret_w_in, ret_gn_g, ret_gn_b, ret_w_o, rwkv_mu, rwkv_w_rkv, rwkv_w0, rwkv_w1, rwkv_w2, rwkv_a0, rwkv_a1, rwkv_a2, rwkv_g1, rwkv_g2, rwkv_k_k, rwkv_k_a, rwkv_r_k, rwkv_gn_g, rwkv_gn_b, rwkv_w_o, rwkv_v0, rwkv_v1, rwkv_v2, ln_mix_g, ln_mix_b, mlp_w1, mlp_w2, ln_mlp_g, ln_mlp_b):
    bn, t, d = x.shape
    m = bn * t
    depth = ln_mix_g.shape[0]
    alpha = (2 * depth) ** 0.25
    assert d % RET_HEAD_DIM == 0 and t % RET_CHUNK == 0 and t % WKV_CHUNK == 0

    xf = x.reshape(m, d).astype(F32)
    xb = xf.astype(BF16)
    cos, sin = _rope_tables(positions)
    v_first = None
    mix_order = jnp.array([0, 2, 3, 1, 4, 5])

    for i in range(depth):
        j = i // N_MIXERS
        if i % N_MIXERS == 0:
            qkvg = _mm2d(xb, ret_w_in, j)
            gated = _retention_core(qkvg, cos, sin, ret_gn_g[j], ret_gn_b[j], bn, t)
            h = _mm2d(gated, ret_w_o, j)
        else:
            x6 = _token_mix(xf, rwkv_mu[j][mix_order], t)
            rkv = _matmul(x6, rwkv_w_rkv.reshape(-1, d, d), g=3, w_off=3 * j)
            lw = _lora(x6, 3, rwkv_w1[j], rwkv_w2[j], "tanh", F32)
            la = _lora(x6, 4, rwkv_a1[j], rwkv_a2[j], None, BF16)
            lg = _lora(x6, 5, rwkv_g1[j], rwkv_g2[j], "sigmoid", BF16)
            if j == 0:
                lv, v0 = None, jnp.zeros((d,), F32)
                v_first = rkv
            else:
                lv = _lora(x6, 2, rwkv_v1[j - 1], rwkv_v2[j - 1], None, BF16)
                v0 = rwkv_v0[j - 1]
            params = (rwkv_w0[j], rwkv_a0[j], rwkv_k_k[j], rwkv_k_a[j], rwkv_r_k[j].reshape(d),
                      rwkv_gn_g[j], rwkv_gn_b[j], v0)
            gated = _wkv_core(rkv, lw, la, lg, lv, v_first, params, bn, t)
            h = _mm2d(gated, rwkv_w_o, j)
        xf, xb = _add_ln(xf, h, ln_mix_g[i], ln_mix_b[i], alpha)
        hid = _mm2d(xb, mlp_w1, i, act="relu2")
        h = _mm2d(hid, mlp_w2, i)
        xf, xb = _add_ln(xf, h, ln_mlp_g[i], ln_mlp_b[i], alpha)
    return xf.reshape(bn, t, d).astype(x.dtype)
```

```python
import functools

import jax
import jax.numpy as jnp
from jax import lax
from jax.experimental import pallas as pl
from jax.experimental.pallas import tpu as pltpu

F32 = jnp.float32
BF16 = jnp.bfloat16

RET_HEAD_DIM = 256
RET_CHUNK = 128
ROPE_BASE = 10000.0
RET_GN_EPS = 1e-5
RWKV_HEAD_DIM = 64
RWKV_GN_EPS = 64e-5
LN_EPS = 1e-5
N_MIXERS = 2

LANES = 128
MXU_DIM = 256
WKV_CHUNK = 64
WKV_GROUP = MXU_DIM // RWKV_HEAD_DIM
WKV_INTERLEAVE = 8
RET_INTERLEAVE = 4
VMEM_LIMIT = 56 * 1024 * 1024


def _cparams(sem, vmem=None):
    return pltpu.CompilerParams(dimension_semantics=sem, vmem_limit_bytes=vmem)


def _round_robin(stage_generators):
    live = list(stage_generators)
    while live:
        nxt = []
        for g in live:
            try:
                next(g)
                nxt.append(g)
            except StopIteration:
                pass
        live = nxt


def _act(x, act):
    if act == "relu2":
        return jnp.square(jnp.maximum(x, 0.0))
    return x


def _mm_kernel(x_ref, w_ref, o_ref, *, act):
    part = jnp.dot(x_ref[...], w_ref[...].astype(BF16), preferred_element_type=F32)
    o_ref[...] = _act(part, act).astype(o_ref.dtype)


def _mm_kt_kernel(x_ref, w_ref, o_ref, acc_ref, *, nk, act):
    part = jnp.dot(x_ref[...], w_ref[...], preferred_element_type=F32)
    k = pl.program_id(2)

    @pl.when(k == 0)
    def _():
        acc_ref[...] = part

    @pl.when(k > 0)
    def _():
        acc_ref[...] += part

    @pl.when(k == nk - 1)
    def _():
        o_ref[...] = _act(acc_ref[...], act).astype(o_ref.dtype)


def _matmul(x, w, *, g=1, w_off=0, act=None, out_dtype=BF16):
    _, m, kd = x.shape
    _, _, n = w.shape
    tm, tn = min(m, 1024), min(n, 512)
    return pl.pallas_call(
        functools.partial(_mm_kernel, act=act),
        grid=(g, m // tm, n // tn),
        in_specs=[pl.BlockSpec((None, tm, kd), lambda b, i, j: (b, i, 0)),
                  pl.BlockSpec((None, kd, tn), lambda b, i, j: (w_off + b, 0, j))],
        out_specs=pl.BlockSpec((None, tm, tn), lambda b, i, j: (b, i, j)),
        out_shape=jax.ShapeDtypeStruct((g, m, n), out_dtype),
        compiler_params=_cparams(("parallel", "parallel", "parallel"), VMEM_LIMIT),
        name="matmul",
    )(x, w)


def _matmul_long_k(x, w, *, act=None, out_dtype=BF16):
    m, kd = x.shape
    n = w.shape[1]
    tm, tn, tk = min(m, 1024), min(n, 1024), min(kd, 4096)
    nk = kd // tk
    return pl.pallas_call(
        functools.partial(_mm_kt_kernel, nk=nk, act=act),
        grid=(m // tm, n // tn, nk),
        in_specs=[pl.BlockSpec((tm, tk), lambda i, j, k: (i, k)),
                  pl.BlockSpec((tk, tn), lambda i, j, k: (k, j))],
        out_specs=pl.BlockSpec((tm, tn), lambda i, j, k: (i, j)),
        out_shape=jax.ShapeDtypeStruct((m, n), out_dtype),
        scratch_shapes=[pltpu.VMEM((tm, tn), F32)],
        compiler_params=_cparams(("parallel", "parallel", "arbitrary"), VMEM_LIMIT),
        name="matmul_long_k",
    )(x, w)


def _mm2d(x, w_stack, layer, **kw):
    return _matmul(x[None], w_stack, w_off=layer, **kw)[0]


def _add_ln_kernel(x_ref, h_ref, g_ref, b_ref, of_ref, ob_ref, *, alpha):
    z = alpha * x_ref[...] + h_ref[...].astype(F32)
    mu = jnp.mean(z, axis=-1, keepdims=True)
    zc = z - mu
    var = jnp.mean(zc * zc, axis=-1, keepdims=True)
    y = zc * lax.rsqrt(var + LN_EPS) * g_ref[...] + b_ref[...]
    of_ref[...] = y
    ob_ref[...] = y.astype(BF16)


def _add_ln(x, h, g, b, alpha):
    m, d = x.shape
    tm = min(m, 256)
    row = pl.BlockSpec((tm, d), lambda i: (i, 0))
    vec = pl.BlockSpec((1, d), lambda i: (0, 0))
    return pl.pallas_call(
        functools.partial(_add_ln_kernel, alpha=alpha),
        grid=(m // tm,),
        in_specs=[row, row, vec, vec],
        out_specs=[row, row],
        out_shape=[jax.ShapeDtypeStruct((m, d), F32), jax.ShapeDtypeStruct((m, d), BF16)],
        compiler_params=_cparams(("parallel",), VMEM_LIMIT),
        name="add_ln",
    )(x, h, g.reshape(1, d), b.reshape(1, d))


def _rope_kernel(pos_ref, freq_ref, cos_ref, sin_ref):
    ang = pos_ref[...].astype(F32) * freq_ref[...]
    cos_ref[...] = jnp.cos(ang)
    sin_ref[...] = jnp.sin(ang)


def _rope_tables(positions):
    m = positions.size
    half = RET_HEAD_DIM // 2
    inv_freq = ROPE_BASE ** (-jnp.arange(half, dtype=F32) / half)
    tm = min(m, 512)
    out = pl.BlockSpec((tm, half), lambda i: (i, 0))
    return pl.pallas_call(
        _rope_kernel,
        grid=(m // tm,),
        in_specs=[pl.BlockSpec((tm, 1), lambda i: (i, 0)),
                  pl.BlockSpec((1, half), lambda i: (0, 0))],
        out_specs=[out, out],
        out_shape=[jax.ShapeDtypeStruct((m, half), F32)] * 2,
        compiler_params=_cparams(("parallel",)),
        name="rope_tables",
    )(positions.reshape(m, 1), inv_freq.reshape(1, half))


def _ret_kernel(q_ref, k_ref, v_ref, g_ref, cos_ref, sin_ref, mask_ref, qd_ref, kd_ref, cd_ref,
                gg_ref, gb_ref, o_ref, state_ref, *, n_chunks, n_heads):
    c_len, dh = RET_CHUNK, RET_HEAD_DIM
    half = dh // 2

    @pl.when(pl.program_id(2) == 0)
    def _():
        state_ref[...] = jnp.zeros_like(state_ref)

    def rot(t, cos, sin):
        t1, t2 = t[:, :half], t[:, half:]
        return jnp.concatenate([t1 * cos - t2 * sin, t1 * sin + t2 * cos], axis=-1)

    def head_chunk(rows, hh, cos, sin):
        cols = slice(hh * dh, (hh + 1) * dh)
        q = rot(q_ref[rows, cols].astype(F32), cos, sin)
        k = rot(k_ref[rows, cols].astype(F32), cos, sin) * (dh ** -0.5)
        v = v_ref[rows, cols]
        inner = lax.dot_general(q.astype(BF16), k.astype(BF16), (((1,), (1,)), ((), ())),
                                preferred_element_type=F32)
        st = state_ref[hh]
        cross = jnp.dot((q * qd_ref[hh]).astype(BF16), st.astype(BF16), preferred_element_type=F32)
        kdt = jnp.transpose(k * kd_ref[hh]).astype(BF16)
        upd = jnp.dot(kdt, v, preferred_element_type=F32)
        yield
        state_ref[hh] = st * cd_ref[hh][0:1, :] + upd
        y = jnp.dot((inner * mask_ref[hh]).astype(BF16), v, preferred_element_type=F32) + cross
        yield
        mu = jnp.mean(y, axis=-1, keepdims=True)
        yc = y - mu
        var = jnp.mean(yc * yc, axis=-1, keepdims=True)
        yn = yc * lax.rsqrt(var + RET_GN_EPS) * gg_ref[:, cols] + gb_ref[:, cols]
        g = g_ref[rows, cols].astype(F32)
        o_ref[rows, cols] = (g * jax.nn.sigmoid(g) * yn).astype(o_ref.dtype)

    def body(c, carry):
        rows = pl.ds(pl.multiple_of(c * c_len, c_len), c_len)
        cos, sin = cos_ref[rows, :], sin_ref[rows, :]
        _round_robin([head_chunk(rows, hh, cos, sin) for hh in range(n_heads)])
        return carry

    lax.fori_loop(0, n_chunks, body, 0)


def _retention_core(qkvg, cos, sin, gn_g, gn_b, bn, t):
    d = qkvg.shape[1] // 4
    h = d // RET_HEAD_DIM
    c_len, dh = RET_CHUNK, RET_HEAD_DIM
    log_g = jnp.log1p(-(2.0 ** (-5.0 - jnp.arange(h, dtype=F32))))
    idx = jnp.arange(c_len, dtype=F32)
    diff = idx[:, None] - idx[None, :]
    mask = jnp.where(diff >= 0, jnp.exp(log_g[:, None, None] * jnp.maximum(diff, 0.0)), 0.0)
    q_decay = jnp.exp(log_g[:, None] * (idx + 1.0))
    k_decay = jnp.exp(log_g[:, None] * (c_len - 1.0 - idx))
    chunk_decay = jnp.exp(log_g * c_len)
    qd = jnp.broadcast_to(q_decay[:, :, None], (h, c_len, dh))
    kd = jnp.broadcast_to(k_decay[:, :, None], (h, c_len, dh))
    cd = jnp.broadcast_to(chunk_decay[:, None, None], (h, 8, dh))

    nh = min(RET_INTERLEAVE, h)
    bw = nh * dh
    rb = min(t, 1024)
    nr = t // rb
    hb = h // nh

    def col(off):
        return pl.BlockSpec((rb, bw), lambda b, j, i, off=off: (b * nr + i, off * hb + j))

    def per_head(rows, width):
        return pl.BlockSpec((nh, rows, width), lambda b, j, i: (j, 0, 0))

    tab = pl.BlockSpec((rb, dh // 2), lambda b, j, i: (b * nr + i, 0))
    vec = pl.BlockSpec((1, bw), lambda b, j, i: (0, j))
    return pl.pallas_call(
        functools.partial(_ret_kernel, n_chunks=rb // c_len, n_heads=nh),
        grid=(bn, hb, nr),
        in_specs=[col(0), col(1), col(2), col(3), tab, tab,
                  per_head(c_len, c_len), per_head(c_len, dh), per_head(c_len, dh), per_head(8, dh),
                  vec, vec],
        out_specs=pl.BlockSpec((rb, bw), lambda b, j, i: (b * nr + i, j)),
        out_shape=jax.ShapeDtypeStruct((bn * t, d), BF16),
        scratch_shapes=[pltpu.VMEM((nh, dh, dh), F32)],
        compiler_params=_cparams(("parallel", "parallel", "arbitrary"), VMEM_LIMIT),
        name="retention",
    )(qkvg, qkvg, qkvg, qkvg, cos, sin, mask, qd, kd, cd, gn_g.reshape(1, d), gn_b.reshape(1, d))


def _mix_kernel(x_ref, prev_ref, mu_ref, o_ref, *, tiles_per_seq):
    x = x_ref[...]
    first = (pl.program_id(0) % tiles_per_seq) == 0
    prev_row = jnp.where(first, 0.0, prev_ref[7:8, :])
    shifted = pltpu.roll(x, 1, axis=0)
    rid = lax.broadcasted_iota(jnp.int32, x.shape, 0)
    xx = jnp.where(rid == 0, prev_row, shifted) - x
    for i in range(o_ref.shape[0]):
        o_ref[i] = (x + xx * mu_ref[i:i + 1, :]).astype(o_ref.dtype)


def _token_mix(x, mu, t):
    m, d = x.shape
    tm = min(t, 256)
    return pl.pallas_call(
        functools.partial(_mix_kernel, tiles_per_seq=t // tm),
        grid=(m // tm,),
        in_specs=[pl.BlockSpec((tm, d), lambda i: (i, 0)),
                  pl.BlockSpec((8, d), lambda i: (jnp.maximum(i * (tm // 8) - 1, 0), 0)),
                  pl.BlockSpec((6, d), lambda i: (0, 0))],
        out_specs=pl.BlockSpec((6, tm, d), lambda i: (0, i, 0)),
        out_shape=jax.ShapeDtypeStruct((6, m, d), BF16),
        compiler_params=_cparams(("parallel",), VMEM_LIMIT),
        name="token_mix",
    )(x, x, mu)


def _lora_kernel(x_ref, w1_ref, w2_ref, o_ref, *, act):
    t = jnp.dot(x_ref[...], w1_ref[...], preferred_element_type=F32)
    if act == "tanh":
        t = jnp.tanh(t)
    elif act == "sigmoid":
        t = jax.nn.sigmoid(t)
    o_ref[...] = jnp.dot(t.astype(BF16), w2_ref[...], preferred_element_type=F32).astype(o_ref.dtype)


def _lora(x6, idx, w1, w2, act, out_dtype):
    _, m, d = x6.shape
    r = w1.shape[1]
    rp = -(-r // LANES) * LANES
    w1p = jnp.pad(w1, ((0, 0), (0, rp - r))).astype(BF16)
    w2p = jnp.pad(w2, ((0, rp - r), (0, 0))).astype(BF16)
    tm = min(m, 512)
    return pl.pallas_call(
        functools.partial(_lora_kernel, act=act),
        grid=(m // tm,),
        in_specs=[pl.BlockSpec((None, tm, d), lambda i: (idx, i, 0)),
                  pl.BlockSpec((d, rp), lambda i: (0, 0)),
                  pl.BlockSpec((rp, d), lambda i: (0, 0))],
        out_specs=pl.BlockSpec((tm, d), lambda i: (i, 0)),
        out_shape=jax.ShapeDtypeStruct((m, d), out_dtype),
        compiler_params=_cparams(("parallel",), VMEM_LIMIT),
        name="lora",
    )(x6, w1p, w2p)


def _split2(x):
    hi = x.astype(BF16)
    return hi, (x - hi.astype(F32)).astype(BF16)


def _block_diag(y, half_masks):
    yb = y.astype(BF16)
    zeros = jnp.zeros((y.shape[0], LANES), BF16)
    blocks = []
    for h in range(WKV_GROUP):
        col = h // 2
        part = yb[:, col * LANES:(col + 1) * LANES] * half_masks[h % 2]
        blocks.append(jnp.concatenate([part, zeros] if col == 0 else [zeros, part], axis=1))
    return jnp.concatenate(blocks, axis=0)


def _wkv_kernel(r_ref, k_ref, v_ref, lw_ref, la_ref, lg_ref, *rest, n_chunks, n_groups, vres):
    if vres:
        lv_ref, vf_ref = rest[0], rest[1]
        rest = rest[2:]
    (w0_ref, a0_ref, kk_ref, ka_ref, rk_ref, gg_ref, gb_ref, v0_ref,
     ones_ref, tri_ref, lo_ref, bdm_ref, o_ref, state_ref) = rest
    L, gw = WKV_CHUNK, MXU_DIM

    @pl.when(pl.program_id(2) == 0)
    def _():
        state_ref[...] = jnp.zeros_like(state_ref)

    lane = lax.broadcasted_iota(jnp.int32, (1, LANES), 1)
    half_masks = [(lane < RWKV_HEAD_DIM).astype(BF16), (lane >= RWKV_HEAD_DIM).astype(BF16)]

    def mm(a, b):
        return jnp.dot(a, b, preferred_element_type=F32)

    def seg_sum(x):
        return mm(x.astype(BF16), ones_ref[...])

    def bd(y):
        return _block_diag(y, half_masks)

    def group_chunk(rows, q):
        cols = slice(q * gw, (q + 1) * gw)
        r = r_ref[rows, cols].astype(F32)
        k0 = k_ref[rows, cols].astype(F32)
        v = v_ref[rows, cols].astype(F32)
        z = -(w0_ref[:, cols] + lw_ref[rows, cols])
        softplus = jnp.maximum(z, 0.0) + jnp.log1p(jnp.exp(-jnp.abs(z)))
        logw = -jnp.exp(-softplus - 0.5)
        a = jax.nn.sigmoid(a0_ref[:, cols] + la_ref[rows, cols].astype(F32))
        if vres:
            v = v + (vf_ref[rows, cols].astype(F32) - v) * jax.nn.sigmoid(
                v0_ref[:, cols] + lv_ref[rows, cols].astype(F32))
        kk = k0 * kk_ref[:, cols]
        ss = seg_sum(kk * kk)
        hi, lo = _split2(logw)
        cum = mm(tri_ref[...], hi) + mm(tri_ref[...], lo)
        yield
        kk = kk * lax.rsqrt(jnp.maximum(ss, 1e-24))
        k = k0 * (1.0 + (a - 1.0) * ka_ref[:, cols])
        avec, bvec = -kk, kk * a
        cum_l = cum[L - 1:L, :]
        e_out = jnp.exp(-cum)
        tail = jnp.exp(cum_l - cum)
        at = (avec * jnp.exp(cum - logw)).astype(BF16)
        rt = (r * jnp.exp(cum)).astype(BF16)
        ar = jnp.concatenate([at, rt], axis=0)
        bk = jnp.concatenate([bd(bvec * e_out), bd(k * e_out)], axis=0)
        p = lax.dot_general(ar, bk, (((1,), (1,)), ((), ())), preferred_element_type=F32)
        st = state_ref[q]
        ph = lax.dot_general(ar, st.astype(BF16), (((1,), (1,)), ((), ())),
                             preferred_element_type=F32)
        bonus_s = seg_sum(r * k * rk_ref[:, cols])
        yield
        strict, incl = lo_ref[0], lo_ref[1]
        m_ab = p[:L, :gw] * strict
        m_ak = p[:L, gw:] * strict
        n_ab = p[L:, :gw] * incl
        n_ak = p[L:, gw:] * incl
        pv = mm(jnp.concatenate([m_ak, n_ak], axis=0).astype(BF16), bd(v))
        x = m_ab
        x2 = mm(x.astype(BF16), bd(x))
        yield
        u = ph[:L] + pv[:L]
        for j in range(6):
            du = mm(x.astype(BF16), bd(u))
            if j < 5:
                x = x2
            if j < 4:
                x2 = mm(x.astype(BF16), bd(x))
            yield
            u = u + du

        yn_u = mm(n_ab.astype(BF16), bd(u))
        uv = jnp.concatenate([u, v], axis=0).astype(BF16)
        bkh = jnp.concatenate([bvec * tail, k * tail], axis=0).astype(BF16)
        upd = lax.dot_general(uv, bkh, (((0,), (0,)), ((), ())), preferred_element_type=F32)
        yield
        state_ref[q] = st * jnp.exp(cum_l) + upd * bdm_ref[...]
        y = ph[L:] + pv[L:] + yn_u
        mu = seg_sum(y) * (1.0 / RWKV_HEAD_DIM)
        yield
        yc = y - mu
        var = seg_sum(yc * yc) * (1.0 / RWKV_HEAD_DIM)
        yield
        yn = yc * lax.rsqrt(var + RWKV_GN_EPS) * gg_ref[:, cols] + gb_ref[:, cols]
        o_ref[rows, cols] = ((yn + bonus_s * v) * lg_ref[rows, cols].astype(F32)).astype(o_ref.dtype)

    def body(c, carry):
        rows = pl.ds(pl.multiple_of(c * L, L), L)
        _round_robin([group_chunk(rows, q) for q in range(n_groups)])
        return carry

    lax.fori_loop(0, n_chunks, body, 0)


def _wkv_core(rkv, lw, la, lg, lv, v_first, params, bn, t):
    _, m, d = rkv.shape
    L, gw = WKV_CHUNK, MXU_DIM
    n_groups = min(WKV_INTERLEAVE, d // gw)
    bw = n_groups * gw
    rb = min(t, 256)
    vres = lv is not None
    hid = jnp.arange(gw) // RWKV_HEAD_DIM
    bdm = (hid[:, None] == hid[None, :])
    ones_bd = bdm.astype(BF16)
    ti = jnp.arange(L)
    tri = (ti[:, None] >= ti[None, :]).astype(BF16)
    si = jnp.arange(gw) % L
    lo_masks = jnp.stack([(si[None, :] < ti[:, None]), (si[None, :] <= ti[:, None])]).astype(F32)

    def act(g=None):
        if g is None:
            return pl.BlockSpec((rb, bw), lambda b, q, i: (b * (t // rb) + i, q))
        return pl.BlockSpec((None, rb, bw), lambda b, q, i, g=g: (g, b * (t // rb) + i, q))

    vec = pl.BlockSpec((1, bw), lambda b, q, i: (0, q))

    def const(shape):
        return pl.BlockSpec(shape, lambda b, q, i: (0,) * len(shape))

    ins = [rkv, rkv, rkv, lw, la, lg]
    specs = [act(0), act(1), act(2), act(), act(), act()]
    if vres:
        ins += [lv, v_first]
        specs += [act(), act(2)]
    ins += [p.reshape(1, d) for p in params]
    specs += [vec] * len(params)
    ins += [ones_bd, tri, lo_masks, bdm.astype(F32)]
    specs += [const((gw, gw)), const((L, L)), const((2, L, gw)), const((gw, gw))]
    return pl.pallas_call(
        functools.partial(_wkv_kernel, n_chunks=rb // L, n_groups=n_groups, vres=vres),
        grid=(bn, d // bw, t // rb),
        in_specs=specs,
        out_specs=act(),
        out_shape=jax.ShapeDtypeStruct((m, d), BF16),
        scratch_shapes=[pltpu.VMEM((n_groups, gw, gw), F32)],
        compiler_params=_cparams(("parallel", "parallel", "arbitrary"), VMEM_LIMIT),
        name="wkv7",
    )(*ins)


def kernel(x, positions, ret_w_in, ret_gn_g, ret_gn_b, ret_w_o, rwkv_mu, rwkv_w_rkv, rwkv_w0, rwkv_w1, rwkv_w2, rwkv_a0, rwkv_a1, rwkv_a2, rwkv_g1, rwkv_g2, rwkv_k_k, rwkv_k_a, rwkv_r_k, rwkv_gn_g, rwkv_gn_b, rwkv_w_o, rwkv_v0, rwkv_v1, rwkv_v2, ln_mix_g, ln_mix_b, mlp_w1, mlp_w2, ln_mlp_g, ln_mlp_b):
    bn, t, d = x.shape
    m = bn * t
    depth = ln_mix_g.shape[0]
    alpha = (2 * depth) ** 0.25
    assert d % RET_HEAD_DIM == 0 and t % RET_CHUNK == 0 and t % WKV_CHUNK == 0

    xf = x.reshape(m, d).astype(F32)
    xb = xf.astype(BF16)
    cos, sin = _rope_tables(positions)
    v_first = None
    mix_order = jnp.array([0, 2, 3, 1, 4, 5])

    for i in range(depth):
        j = i // N_MIXERS
        if i % N_MIXERS == 0:
            qkvg = _mm2d(xb, ret_w_in, j)
            gated = _retention_core(qkvg, cos, sin, ret_gn_g[j], ret_gn_b[j], bn, t)
            h = _mm2d(gated, ret_w_o, j)
        else:
            x6 = _token_mix(xf, rwkv_mu[j][mix_order], t)
            rkv = _matmul(x6, rwkv_w_rkv.reshape(-1, d, d), g=3, w_off=3 * j)
            lw = _lora(x6, 3, rwkv_w1[j], rwkv_w2[j], "tanh", F32)
            la = _lora(x6, 4, rwkv_a1[j], rwkv_a2[j], None, BF16)
            lg = _lora(x6, 5, rwkv_g1[j], rwkv_g2[j], "sigmoid", BF16)
            if j == 0:
                lv, v0 = None, jnp.zeros((d,), F32)
                v_first = rkv
            else:
                lv = _lora(x6, 2, rwkv_v1[j - 1], rwkv_v2[j - 1], None, BF16)
                v0 = rwkv_v0[j - 1]
            params = (rwkv_w0[j], rwkv_a0[j], rwkv_k_k[j], rwkv_k_a[j], rwkv_r_k[j].reshape(d),
                      rwkv_gn_g[j], rwkv_gn_b[j], v0)
            gated = _wkv_core(rkv, lw, la, lg, lv, v_first, params, bn, t)
            h = _mm2d(gated, rwkv_w_o, j)
        xf, xb = _add_ln(xf, h, ln_mix_g[i], ln_mix_b[i], alpha)
        hid = _mm2d(xb, mlp_w1, i, act="relu2")
        h = _matmul_long_k(hid, mlp_w2[i].astype(BF16))
        xf, xb = _add_ln(xf, h, ln_mlp_g[i], ln_mlp_b[i], alpha)
    return xf.reshape(bn, t, d).astype(x.dtype)
```

```python
import functools

import jax
import jax.numpy as jnp
from jax import lax
from jax.experimental import pallas as pl
from jax.experimental.pallas import tpu as pltpu

F32 = jnp.float32
BF16 = jnp.bfloat16

RET_HEAD_DIM = 256
RET_CHUNK = 128
ROPE_BASE = 10000.0
RET_GN_EPS = 1e-5
RWKV_HEAD_DIM = 64
RWKV_GN_EPS = 64e-5
LN_EPS = 1e-5
N_MIXERS = 2

LANES = 128
MXU_DIM = 256
WKV_CHUNK = 64
WKV_GROUP = MXU_DIM // RWKV_HEAD_DIM
WKV_INTERLEAVE = 8
RET_INTERLEAVE = 4
VMEM_LIMIT = 56 * 1024 * 1024


def _cparams(sem, vmem=None):
    return pltpu.CompilerParams(dimension_semantics=sem, vmem_limit_bytes=vmem)


def _round_robin(stage_generators):
    live = list(stage_generators)
    while live:
        nxt = []
        for g in live:
            try:
                next(g)
                nxt.append(g)
            except StopIteration:
                pass
        live = nxt


def _act(x, act):
    if act == "relu2":
        return jnp.square(jnp.maximum(x, 0.0))
    return x


def _mm_kernel(x_ref, w_ref, o_ref, *, act):
    part = jnp.dot(x_ref[...], w_ref[...].astype(BF16), preferred_element_type=F32)
    o_ref[...] = _act(part, act).astype(o_ref.dtype)


def _mm_kt_kernel(x_ref, w_ref, o_ref, acc_ref, *, nk, act):
    part = jnp.dot(x_ref[...], w_ref[...], preferred_element_type=F32)
    k = pl.program_id(2)

    @pl.when(k == 0)
    def _():
        acc_ref[...] = part

    @pl.when(k > 0)
    def _():
        acc_ref[...] += part

    @pl.when(k == nk - 1)
    def _():
        o_ref[...] = _act(acc_ref[...], act).astype(o_ref.dtype)


def _matmul(x, w, *, g=1, w_off=0, act=None, out_dtype=BF16):
    _, m, kd = x.shape
    _, _, n = w.shape
    tm, tn = min(m, 1024), min(n, 512)
    return pl.pallas_call(
        functools.partial(_mm_kernel, act=act),
        grid=(g, m // tm, n // tn),
        in_specs=[pl.BlockSpec((None, tm, kd), lambda b, i, j: (b, i, 0)),
                  pl.BlockSpec((None, kd, tn), lambda b, i, j: (w_off + b, 0, j))],
        out_specs=pl.BlockSpec((None, tm, tn), lambda b, i, j: (b, i, j)),
        out_shape=jax.ShapeDtypeStruct((g, m, n), out_dtype),
        compiler_params=_cparams(("parallel", "parallel", "parallel"), VMEM_LIMIT),
        name="matmul",
    )(x, w)


def _matmul_long_k(x, w, *, act=None, out_dtype=BF16):
    m, kd = x.shape
    n = w.shape[1]
    tm, tn, tk = min(m, 1024), min(n, 1024), min(kd, 4096)
    nk = kd // tk
    return pl.pallas_call(
        functools.partial(_mm_kt_kernel, nk=nk, act=act),
        grid=(m // tm, n // tn, nk),
        in_specs=[pl.BlockSpec((tm, tk), lambda i, j, k: (i, k)),
                  pl.BlockSpec((tk, tn), lambda i, j, k: (k, j))],
        out_specs=pl.BlockSpec((tm, tn), lambda i, j, k: (i, j)),
        out_shape=jax.ShapeDtypeStruct((m, n), out_dtype),
        scratch_shapes=[pltpu.VMEM((tm, tn), F32)],
        compiler_params=_cparams(("parallel", "parallel", "arbitrary"), VMEM_LIMIT),
        name="matmul_long_k",
    )(x, w)


def _mm2d(x, w_stack, layer, **kw):
    return _matmul(x[None], w_stack, w_off=layer, **kw)[0]


def _add_ln_kernel(x_ref, h_ref, g_ref, b_ref, of_ref, ob_ref, *, alpha):
    z = alpha * x_ref[...] + h_ref[...].astype(F32)
    mu = jnp.mean(z, axis=-1, keepdims=True)
    zc = z - mu
    var = jnp.mean(zc * zc, axis=-1, keepdims=True)
    y = zc * lax.rsqrt(var + LN_EPS) * g_ref[...] + b_ref[...]
    of_ref[...] = y
    ob_ref[...] = y.astype(BF16)


def _add_ln(x, h, g, b, alpha):
    m, d = x.shape
    tm = min(m, 256)
    row = pl.BlockSpec((tm, d), lambda i: (i, 0))
    vec = pl.BlockSpec((1, d), lambda i: (0, 0))
    return pl.pallas_call(
        functools.partial(_add_ln_kernel, alpha=alpha),
        grid=(m // tm,),
        in_specs=[row, row, vec, vec],
        out_specs=[row, row],
        out_shape=[jax.ShapeDtypeStruct((m, d), F32), jax.ShapeDtypeStruct((m, d), BF16)],
        compiler_params=_cparams(("parallel",), VMEM_LIMIT),
        name="add_ln",
    )(x, h, g.reshape(1, d), b.reshape(1, d))


def _rope_kernel(pos_ref, freq_ref, cos_ref, sin_ref):
    ang = pos_ref[...].astype(F32) * freq_ref[...]
    cos_ref[...] = jnp.cos(ang)
    sin_ref[...] = jnp.sin(ang)


def _rope_tables(positions):
    m = positions.size
    half = RET_HEAD_DIM // 2
    inv_freq = ROPE_BASE ** (-jnp.arange(half, dtype=F32) / half)
    tm = min(m, 512)
    out = pl.BlockSpec((tm, half), lambda i: (i, 0))
    return pl.pallas_call(
        _rope_kernel,
        grid=(m // tm,),
        in_specs=[pl.BlockSpec((tm, 1), lambda i: (i, 0)),
                  pl.BlockSpec((1, half), lambda i: (0, 0))],
        out_specs=[out, out],
        out_shape=[jax.ShapeDtypeStruct((m, half), F32)] * 2,
        compiler_params=_cparams(("parallel",)),
        name="rope_tables",
    )(positions.reshape(m, 1), inv_freq.reshape(1, half))


def _ret_kernel(q_ref, k_ref, v_ref, g_ref, cos_ref, sin_ref, mask_ref, qd_ref, kd_ref, cd_ref,
                gg_ref, gb_ref, o_ref, state_ref, *, n_chunks, n_heads):
    c_len, dh = RET_CHUNK, RET_HEAD_DIM
    half = dh // 2

    @pl.when(pl.program_id(2) == 0)
    def _():
        state_ref[...] = jnp.zeros_like(state_ref)

    def rot(t, cos, sin):
        t1, t2 = t[:, :half], t[:, half:]
        return jnp.concatenate([t1 * cos - t2 * sin, t1 * sin + t2 * cos], axis=-1)

    def head_chunk(rows, hh, cos, sin):
        cols = slice(hh * dh, (hh + 1) * dh)
        q = rot(q_ref[rows, cols].astype(F32), cos, sin)
        k = rot(k_ref[rows, cols].astype(F32), cos, sin) * (dh ** -0.5)
        v = v_ref[rows, cols]
        inner = lax.dot_general(q.astype(BF16), k.astype(BF16), (((1,), (1,)), ((), ())),
                                preferred_element_type=F32)
        st = state_ref[hh]
        cross = jnp.dot((q * qd_ref[hh]).astype(BF16), st.astype(BF16), preferred_element_type=F32)
        kdt = jnp.transpose(k * kd_ref[hh]).astype(BF16)
        upd = jnp.dot(kdt, v, preferred_element_type=F32)
        yield
        state_ref[hh] = st * cd_ref[hh][0:1, :] + upd
        y = jnp.dot((inner * mask_ref[hh]).astype(BF16), v, preferred_element_type=F32) + cross
        yield
        mu = jnp.mean(y, axis=-1, keepdims=True)
        yc = y - mu
        var = jnp.mean(yc * yc, axis=-1, keepdims=True)
        yn = yc * lax.rsqrt(var + RET_GN_EPS) * gg_ref[:, cols] + gb_ref[:, cols]
        g = g_ref[rows, cols].astype(F32)
        o_ref[rows, cols] = (g * jax.nn.sigmoid(g) * yn).astype(o_ref.dtype)

    def body(c, carry):
        rows = pl.ds(pl.multiple_of(c * c_len, c_len), c_len)
        cos, sin = cos_ref[rows, :], sin_ref[rows, :]
        _round_robin([head_chunk(rows, hh, cos, sin) for hh in range(n_heads)])
        return carry

    lax.fori_loop(0, n_chunks, body, 0)


def _retention_core(qkvg, cos, sin, gn_g, gn_b, bn, t):
    d = qkvg.shape[1] // 4
    h = d // RET_HEAD_DIM
    c_len, dh = RET_CHUNK, RET_HEAD_DIM
    log_g = jnp.log1p(-(2.0 ** (-5.0 - jnp.arange(h, dtype=F32))))
    idx = jnp.arange(c_len, dtype=F32)
    diff = idx[:, None] - idx[None, :]
    mask = jnp.where(diff >= 0, jnp.exp(log_g[:, None, None] * jnp.maximum(diff, 0.0)), 0.0)
    q_decay = jnp.exp(log_g[:, None] * (idx + 1.0))
    k_decay = jnp.exp(log_g[:, None] * (c_len - 1.0 - idx))
    chunk_decay = jnp.exp(log_g * c_len)
    qd = jnp.broadcast_to(q_decay[:, :, None], (h, c_len, dh))
    kd = jnp.broadcast_to(k_decay[:, :, None], (h, c_len, dh))
    cd = jnp.broadcast_to(chunk_decay[:, None, None], (h, 8, dh))

    nh = min(RET_INTERLEAVE, h)
    bw = nh * dh
    rb = min(t, 1024)
    nr = t // rb
    hb = h // nh

    def col(off):
        return pl.BlockSpec((rb, bw), lambda b, j, i, off=off: (b * nr + i, off * hb + j))

    def per_head(rows, width):
        return pl.BlockSpec((nh, rows, width), lambda b, j, i: (j, 0, 0))

    tab = pl.BlockSpec((rb, dh // 2), lambda b, j, i: (b * nr + i, 0))
    vec = pl.BlockSpec((1, bw), lambda b, j, i: (0, j))
    return pl.pallas_call(
        functools.partial(_ret_kernel, n_chunks=rb // c_len, n_heads=nh),
        grid=(bn, hb, nr),
        in_specs=[col(0), col(1), col(2), col(3), tab, tab,
                  per_head(c_len, c_len), per_head(c_len, dh), per_head(c_len, dh), per_head(8, dh),
                  vec, vec],
        out_specs=pl.BlockSpec((rb, bw), lambda b, j, i: (b * nr + i, j)),
        out_shape=jax.ShapeDtypeStruct((bn * t, d), BF16),
        scratch_shapes=[pltpu.VMEM((nh, dh, dh), F32)],
        compiler_params=_cparams(("parallel", "parallel", "arbitrary"), VMEM_LIMIT),
        name="retention",
    )(qkvg, qkvg, qkvg, qkvg, cos, sin, mask, qd, kd, cd, gn_g.reshape(1, d), gn_b.reshape(1, d))


def _lora_act(t, act):
    if act == "tanh":
        return jnp.tanh(t)
    if act == "sigmoid":
        return jax.nn.sigmoid(t)
    return t


def _mix_kernel(x_ref, prev_ref, mu_ref, w1_ref, o_ref, t_ref, *, tiles_per_seq, loras):
    x = x_ref[...]
    first = (pl.program_id(0) % tiles_per_seq) == 0
    prev_row = jnp.where(first, 0.0, prev_ref[7:8, :])
    shifted = pltpu.roll(x, 1, axis=0)
    rid = lax.broadcasted_iota(jnp.int32, x.shape, 0)
    xx = jnp.where(rid == 0, prev_row, shifted) - x
    for i in range(o_ref.shape[0]):
        o_ref[i] = (x + xx * mu_ref[i:i + 1, :]).astype(o_ref.dtype)
    for mix, act, lo, hi in loras:
        xm = (x + xx * mu_ref[mix:mix + 1, :]).astype(BF16)
        down = jnp.dot(xm, w1_ref[:, lo:hi], preferred_element_type=F32)
        t_ref[:, lo:hi] = _lora_act(down, act).astype(t_ref.dtype)


def _token_mix(x, mu, w1cat, loras, t):
    m, d = x.shape
    rt = w1cat.shape[1]
    tm = min(t, 256)
    return pl.pallas_call(
        functools.partial(_mix_kernel, tiles_per_seq=t // tm, loras=loras),
        grid=(m // tm,),
        in_specs=[pl.BlockSpec((tm, d), lambda i: (i, 0)),
                  pl.BlockSpec((8, d), lambda i: (jnp.maximum(i * (tm // 8) - 1, 0), 0)),
                  pl.BlockSpec((6, d), lambda i: (0, 0)),
                  pl.BlockSpec((d, rt), lambda i: (0, 0))],
        out_specs=[pl.BlockSpec((3, tm, d), lambda i: (0, i, 0)),
                   pl.BlockSpec((tm, rt), lambda i: (i, 0))],
        out_shape=[jax.ShapeDtypeStruct((3, m, d), BF16), jax.ShapeDtypeStruct((m, rt), BF16)],
        compiler_params=_cparams(("parallel",), VMEM_LIMIT),
        name="token_mix",
    )(x, x, mu, w1cat)


def _lora_up_kernel(t_ref, *refs, ranges):
    n = len(ranges)
    for (lo, hi), w2_ref, o_ref in zip(ranges, refs[:n], refs[n:]):
        o_ref[...] = jnp.dot(t_ref[:, lo:hi], w2_ref[...], preferred_element_type=F32).astype(o_ref.dtype)


def _pad_rank(r):
    return -(-r // LANES) * LANES


def _lora_prepare(specs):
    w1s, w2s, loras, col = [], [], [], 0
    for mix, w1, w2, act, _ in specs:
        r = w1.shape[1]
        rp = _pad_rank(r)
        w1s.append(jnp.pad(w1, ((0, 0), (0, rp - r))).astype(BF16))
        w2s.append(jnp.pad(w2, ((0, rp - r), (0, 0))).astype(BF16))
        loras.append((mix, act, col, col + rp))
        col += rp
    return jnp.concatenate(w1s, axis=1), w2s, tuple(loras)


def _lora_up(tact, w2s, loras, out_dtypes):
    m, rt = tact.shape
    d = w2s[0].shape[1]
    tm = min(m, 256)
    ranges = tuple((lo, hi) for _, _, lo, hi in loras)
    row = pl.BlockSpec((tm, d), lambda i: (i, 0))
    return pl.pallas_call(
        functools.partial(_lora_up_kernel, ranges=ranges),
        grid=(m // tm,),
        in_specs=[pl.BlockSpec((tm, rt), lambda i: (i, 0))]
                 + [pl.BlockSpec(w.shape, lambda i: (0, 0)) for w in w2s],
        out_specs=[row] * len(w2s),
        out_shape=[jax.ShapeDtypeStruct((m, d), dt) for dt in out_dtypes],
        compiler_params=_cparams(("parallel",), VMEM_LIMIT),
        name="lora_up",
    )(tact, *w2s)


def _split2(x):
    hi = x.astype(BF16)
    return hi, (x - hi.astype(F32)).astype(BF16)


def _block_diag(y, half_masks):
    yb = y.astype(BF16)
    zeros = jnp.zeros((y.shape[0], LANES), BF16)
    blocks = []
    for h in range(WKV_GROUP):
        col = h // 2
        part = yb[:, col * LANES:(col + 1) * LANES] * half_masks[h % 2]
        blocks.append(jnp.concatenate([part, zeros] if col == 0 else [zeros, part], axis=1))
    return jnp.concatenate(blocks, axis=0)


def _wkv_kernel(r_ref, k_ref, v_ref, lw_ref, la_ref, lg_ref, *rest, n_chunks, n_groups, vres):
    if vres:
        lv_ref, vf_ref = rest[0], rest[1]
        rest = rest[2:]
    (w0_ref, a0_ref, kk_ref, ka_ref, rk_ref, gg_ref, gb_ref, v0_ref,
     ones_ref, tri_ref, lo_ref, bdm_ref, o_ref, state_ref) = rest
    L, gw = WKV_CHUNK, MXU_DIM

    @pl.when(pl.program_id(2) == 0)
    def _():
        state_ref[...] = jnp.zeros_like(state_ref)

    lane = lax.broadcasted_iota(jnp.int32, (1, LANES), 1)
    half_masks = [(lane < RWKV_HEAD_DIM).astype(BF16), (lane >= RWKV_HEAD_DIM).astype(BF16)]

    def mm(a, b):
        return jnp.dot(a, b, preferred_element_type=F32)

    def seg_sum(x):
        return mm(x.astype(BF16), ones_ref[...])

    def bd(y):
        return _block_diag(y, half_masks)

    def group_chunk(rows, q):
        cols = slice(q * gw, (q + 1) * gw)
        r = r_ref[rows, cols].astype(F32)
        k0 = k_ref[rows, cols].astype(F32)
        v = v_ref[rows, cols].astype(F32)
        z = -(w0_ref[:, cols] + lw_ref[rows, cols])
        softplus = jnp.maximum(z, 0.0) + jnp.log1p(jnp.exp(-jnp.abs(z)))
        logw = -jnp.exp(-softplus - 0.5)
        a = jax.nn.sigmoid(a0_ref[:, cols] + la_ref[rows, cols].astype(F32))
        if vres:
            v = v + (vf_ref[rows, cols].astype(F32) - v) * jax.nn.sigmoid(
                v0_ref[:, cols] + lv_ref[rows, cols].astype(F32))
        kk = k0 * kk_ref[:, cols]
        ss = seg_sum(kk * kk)
        hi, lo = _split2(logw)
        cum = mm(tri_ref[...], hi) + mm(tri_ref[...], lo)
        yield
        kk = kk * lax.rsqrt(jnp.maximum(ss, 1e-24))
        k = k0 * (1.0 + (a - 1.0) * ka_ref[:, cols])
        avec, bvec = -kk, kk * a
        cum_l = cum[L - 1:L, :]
        e_out = jnp.exp(-cum)
        tail = jnp.exp(cum_l - cum)
        at = (avec * jnp.exp(cum - logw)).astype(BF16)
        rt = (r * jnp.exp(cum)).astype(BF16)
        ar = jnp.concatenate([at, rt], axis=0)
        bk = jnp.concatenate([bd(bvec * e_out), bd(k * e_out)], axis=0)
        p = lax.dot_general(ar, bk, (((1,), (1,)), ((), ())), preferred_element_type=F32)
        st = state_ref[q]
        ph = lax.dot_general(ar, st.astype(BF16), (((1,), (1,)), ((), ())),
                             preferred_element_type=F32)
        bonus_s = seg_sum(r * k * rk_ref[:, cols])
        yield
        strict, incl = lo_ref[0], lo_ref[1]
        m_ab = p[:L, :gw] * strict
        m_ak = p[:L, gw:] * strict
        n_ab = p[L:, :gw] * incl
        n_ak = p[L:, gw:] * incl
        pv = mm(jnp.concatenate([m_ak, n_ak], axis=0).astype(BF16), bd(v))
        x = m_ab
        x2 = mm(x.astype(BF16), bd(x))
        yield
        u = ph[:L] + pv[:L]
        for j in range(6):
            du = mm(x.astype(BF16), bd(u))
            if j < 5:
                x = x2
            if j < 4:
                x2 = mm(x.astype(BF16), bd(x))
            yield
            u = u + du

        yn_u = mm(n_ab.astype(BF16), bd(u))
        uv = jnp.concatenate([u, v], axis=0).astype(BF16)
        bkh = jnp.concatenate([bvec * tail, k * tail], axis=0).astype(BF16)
        upd = lax.dot_general(uv, bkh, (((0,), (0,)), ((), ())), preferred_element_type=F32)
        yield
        state_ref[q] = st * jnp.exp(cum_l) + upd * bdm_ref[...]
        y = ph[L:] + pv[L:] + yn_u
        mu = seg_sum(y) * (1.0 / RWKV_HEAD_DIM)
        yield
        yc = y - mu
        var = seg_sum(yc * yc) * (1.0 / RWKV_HEAD_DIM)
        yield
        yn = yc * lax.rsqrt(var + RWKV_GN_EPS) * gg_ref[:, cols] + gb_ref[:, cols]
        o_ref[rows, cols] = ((yn + bonus_s * v) * lg_ref[rows, cols].astype(F32)).astype(o_ref.dtype)

    def body(c, carry):
        rows = pl.ds(pl.multiple_of(c * L, L), L)
        _round_robin([group_chunk(rows, q) for q in range(n_groups)])
        return carry

    lax.fori_loop(0, n_chunks, body, 0)


def _wkv_core(rkv, lw, la, lg, lv, v_first, params, bn, t):
    _, m, d = rkv.shape
    L, gw = WKV_CHUNK, MXU_DIM
    n_groups = min(WKV_INTERLEAVE, d // gw)
    bw = n_groups * gw
    rb = min(t, 256)
    vres = lv is not None
    hid = jnp.arange(gw) // RWKV_HEAD_DIM
    bdm = (hid[:, None] == hid[None, :])
    ones_bd = bdm.astype(BF16)
    ti = jnp.arange(L)
    tri = (ti[:, None] >= ti[None, :]).astype(BF16)
    si = jnp.arange(gw) % L
    lo_masks = jnp.stack([(si[None, :] < ti[:, None]), (si[None, :] <= ti[:, None])]).astype(F32)

    def act(g=None):
        if g is None:
            return pl.BlockSpec((rb, bw), lambda b, q, i: (b * (t // rb) + i, q))
        return pl.BlockSpec((None, rb, bw), lambda b, q, i, g=g: (g, b * (t // rb) + i, q))

    vec = pl.BlockSpec((1, bw), lambda b, q, i: (0, q))

    def const(shape):
        return pl.BlockSpec(shape, lambda b, q, i: (0,) * len(shape))

    ins = [rkv, rkv, rkv, lw, la, lg]
    specs = [act(0), act(1), act(2), act(), act(), act()]
    if vres:
        ins += [lv, v_first]
        specs += [act(), act(2)]
    ins += [p.reshape(1, d) for p in params]
    specs += [vec] * len(params)
    ins += [ones_bd, tri, lo_masks, bdm.astype(F32)]
    specs += [const((gw, gw)), const((L, L)), const((2, L, gw)), const((gw, gw))]
    return pl.pallas_call(
        functools.partial(_wkv_kernel, n_chunks=rb // L, n_groups=n_groups, vres=vres),
        grid=(bn, d // bw, t // rb),
        in_specs=specs,
        out_specs=act(),
        out_shape=jax.ShapeDtypeStruct((m, d), BF16),
        scratch_shapes=[pltpu.VMEM((n_groups, gw, gw), F32)],
        compiler_params=_cparams(("parallel", "parallel", "arbitrary"), VMEM_LIMIT),
        name="wkv7",
    )(*ins)


def kernel(x, positions, ret_w_in, ret_gn_g, ret_gn_b, ret_w_o, rwkv_mu, rwkv_w_rkv, rwkv_w0, rwkv_w1, rwkv_w2, rwkv_a0, rwkv_a1, rwkv_a2, rwkv_g1, rwkv_g2, rwkv_k_k, rwkv_k_a, rwkv_r_k, rwkv_gn_g, rwkv_gn_b, rwkv_w_o, rwkv_v0, rwkv_v1, rwkv_v2, ln_mix_g, ln_mix_b, mlp_w1, mlp_w2, ln_mlp_g, ln_mlp_b):
    bn, t, d = x.shape
    m = bn * t
    depth = ln_mix_g.shape[0]
    alpha = (2 * depth) ** 0.25
    assert d % RET_HEAD_DIM == 0 and t % RET_CHUNK == 0 and t % WKV_CHUNK == 0

    xf = x.reshape(m, d).astype(F32)
    xb = xf.astype(BF16)
    cos, sin = _rope_tables(positions)
    v_first = None
    mix_order = jnp.array([0, 2, 3, 1, 4, 5])

    for i in range(depth):
        j = i // N_MIXERS
        if i % N_MIXERS == 0:
            qkvg = _mm2d(xb, ret_w_in, j)
            gated = _retention_core(qkvg, cos, sin, ret_gn_g[j], ret_gn_b[j], bn, t)
            h = _mm2d(gated, ret_w_o, j)
        else:
            specs = [(3, rwkv_w1[j], rwkv_w2[j], "tanh", F32),
                     (4, rwkv_a1[j], rwkv_a2[j], None, BF16),
                     (5, rwkv_g1[j], rwkv_g2[j], "sigmoid", BF16)]
            if j > 0:
                specs.append((2, rwkv_v1[j - 1], rwkv_v2[j - 1], None, BF16))
            w1cat, w2s, loras = _lora_prepare(specs)
            x3, tact = _token_mix(xf, rwkv_mu[j][mix_order], w1cat, loras, t)
            rkv = _matmul(x3, rwkv_w_rkv.reshape(-1, d, d), g=3, w_off=3 * j)
            ups = _lora_up(tact, w2s, loras, [s[4] for s in specs])
            lw, la, lg = ups[:3]
            if j == 0:
                lv, v0 = None, jnp.zeros((d,), F32)
                v_first = rkv
            else:
                lv, v0 = ups[3], rwkv_v0[j - 1]
            params = (rwkv_w0[j], rwkv_a0[j], rwkv_k_k[j], rwkv_k_a[j], rwkv_r_k[j].reshape(d),
                      rwkv_gn_g[j], rwkv_gn_b[j], v0)
            gated = _wkv_core(rkv, lw, la, lg, lv, v_first, params, bn, t)
            h = _mm2d(gated, rwkv_w_o, j)
        xf, xb = _add_ln(xf, h, ln_mix_g[i], ln_mix_b[i], alpha)
        hid = _mm2d(xb, mlp_w1, i, act="relu2")
        h = _matmul_long_k(hid, mlp_w2[i].astype(BF16))
        xf, xb = _add_ln(xf, h, ln_mlp_g[i], ln_mlp_b[i], alpha)
    return xf.reshape(bn, t, d).astype(x.dtype)
```

```python
import functools

import jax
import jax.numpy as jnp
from jax import lax
from jax.experimental import pallas as pl
from jax.experimental.pallas import tpu as pltpu

F32 = jnp.float32
BF16 = jnp.bfloat16

RET_HEAD_DIM = 256
RET_CHUNK = 128
ROPE_BASE = 10000.0
RET_GN_EPS = 1e-5
RWKV_HEAD_DIM = 64
RWKV_GN_EPS = 64e-5
LN_EPS = 1e-5
N_MIXERS = 2

LANES = 128
BF16_SUBLANES = 16
MXU_DIM = 256
WKV_CHUNK = 64
WKV_GROUP = MXU_DIM // RWKV_HEAD_DIM
WKV_INTERLEAVE = 8
RET_INTERLEAVE = 4
VMEM_LIMIT = 56 * 1024 * 1024


def _cparams(sem, vmem=None):
    return pltpu.CompilerParams(dimension_semantics=sem, vmem_limit_bytes=vmem)


def _round_robin(stage_generators):
    live = list(stage_generators)
    while live:
        nxt = []
        for g in live:
            try:
                next(g)
                nxt.append(g)
            except StopIteration:
                pass
        live = nxt


def _act(x, act):
    if act == "relu2":
        return jnp.square(jnp.maximum(x, 0.0))
    return x


def _mm_kernel(x_ref, w_ref, o_ref, *, act):
    part = jnp.dot(x_ref[...], w_ref[...].astype(BF16), preferred_element_type=F32)
    o_ref[...] = _act(part, act).astype(o_ref.dtype)


def _mm_kt_kernel(x_ref, w_ref, o_ref, acc_ref, *, nk, act):
    part = jnp.dot(x_ref[...], w_ref[...], preferred_element_type=F32)
    k = pl.program_id(2)

    @pl.when(k == 0)
    def _():
        acc_ref[...] = part

    @pl.when(k > 0)
    def _():
        acc_ref[...] += part

    @pl.when(k == nk - 1)
    def _():
        o_ref[...] = _act(acc_ref[...], act).astype(o_ref.dtype)


def _matmul(x, w, *, g=1, w_off=0, act=None, out_dtype=BF16):
    _, m, kd = x.shape
    _, _, n = w.shape
    tm, tn = min(m, 1024), min(n, 512)
    return pl.pallas_call(
        functools.partial(_mm_kernel, act=act),
        grid=(g, m // tm, n // tn),
        in_specs=[pl.BlockSpec((None, tm, kd), lambda b, i, j: (b, i, 0)),
                  pl.BlockSpec((None, kd, tn), lambda b, i, j: (w_off + b, 0, j))],
        out_specs=pl.BlockSpec((None, tm, tn), lambda b, i, j: (b, i, j)),
        out_shape=jax.ShapeDtypeStruct((g, m, n), out_dtype),
        compiler_params=_cparams(("parallel", "parallel", "parallel"), VMEM_LIMIT),
        name="matmul",
    )(x, w)


def _matmul_long_k(x, w, *, act=None, out_dtype=BF16):
    m, kd = x.shape
    n = w.shape[1]
    tm, tn, tk = min(m, 1024), min(n, 1024), min(kd, 4096)
    nk = kd // tk
    return pl.pallas_call(
        functools.partial(_mm_kt_kernel, nk=nk, act=act),
        grid=(m // tm, n // tn, nk),
        in_specs=[pl.BlockSpec((tm, tk), lambda i, j, k: (i, k)),
                  pl.BlockSpec((tk, tn), lambda i, j, k: (k, j))],
        out_specs=pl.BlockSpec((tm, tn), lambda i, j, k: (i, j)),
        out_shape=jax.ShapeDtypeStruct((m, n), out_dtype),
        scratch_shapes=[pltpu.VMEM((tm, tn), F32)],
        compiler_params=_cparams(("parallel", "parallel", "arbitrary"), VMEM_LIMIT),
        name="matmul_long_k",
    )(x, w)


def _mm2d(x, w_stack, layer, **kw):
    return _matmul(x[None], w_stack, w_off=layer, **kw)[0]


def _add_ln_kernel(x_ref, h_ref, g_ref, b_ref, o_ref, *, alpha):
    z = alpha * x_ref[...].astype(F32) + h_ref[...].astype(F32)
    mu = jnp.mean(z, axis=-1, keepdims=True)
    zc = z - mu
    var = jnp.mean(zc * zc, axis=-1, keepdims=True)
    y = zc * lax.rsqrt(var + LN_EPS) * g_ref[...] + b_ref[...]
    o_ref[...] = y.astype(o_ref.dtype)


def _add_ln(x, h, g, b, alpha, out_dtype):
    m, d = x.shape
    tm = min(m, 256)
    row = pl.BlockSpec((tm, d), lambda i: (i, 0))
    vec = pl.BlockSpec((1, d), lambda i: (0, 0))
    return pl.pallas_call(
        functools.partial(_add_ln_kernel, alpha=alpha),
        grid=(m // tm,),
        in_specs=[row, row, vec, vec],
        out_specs=row,
        out_shape=jax.ShapeDtypeStruct((m, d), out_dtype),
        compiler_params=_cparams(("parallel",), VMEM_LIMIT),
        name="add_ln",
    )(x, h, g.reshape(1, d), b.reshape(1, d))


def _rope_kernel(pos_ref, freq_ref, cos_ref, sin_ref):
    ang = pos_ref[...].astype(F32) * freq_ref[...]
    cos_ref[...] = jnp.cos(ang)
    sin_ref[...] = jnp.sin(ang)


def _rope_tables(positions):
    m = positions.size
    half = RET_HEAD_DIM // 2
    inv_freq = ROPE_BASE ** (-jnp.arange(half, dtype=F32) / half)
    tm = min(m, 512)
    out = pl.BlockSpec((tm, half), lambda i: (i, 0))
    return pl.pallas_call(
        _rope_kernel,
        grid=(m // tm,),
        in_specs=[pl.BlockSpec((tm, 1), lambda i: (i, 0)),
                  pl.BlockSpec((1, half), lambda i: (0, 0))],
        out_specs=[out, out],
        out_shape=[jax.ShapeDtypeStruct((m, half), F32)] * 2,
        compiler_params=_cparams(("parallel",)),
        name="rope_tables",
    )(positions.reshape(m, 1), inv_freq.reshape(1, half))


def _ret_kernel(q_ref, k_ref, v_ref, g_ref, cos_ref, sin_ref, mask_ref, qd_ref, kd_ref, cd_ref,
                gg_ref, gb_ref, o_ref, state_ref, *, n_chunks, n_heads):
    c_len, dh = RET_CHUNK, RET_HEAD_DIM
    half = dh // 2

    @pl.when(pl.program_id(2) == 0)
    def _():
        state_ref[...] = jnp.zeros_like(state_ref)

    def rot(t, cos, sin):
        t1, t2 = t[:, :half], t[:, half:]
        return jnp.concatenate([t1 * cos - t2 * sin, t1 * sin + t2 * cos], axis=-1)

    def head_chunk(rows, hh, cos, sin):
        cols = slice(hh * dh, (hh + 1) * dh)
        q = rot(q_ref[rows, cols].astype(F32), cos, sin)
        k = rot(k_ref[rows, cols].astype(F32), cos, sin) * (dh ** -0.5)
        v = v_ref[rows, cols]
        inner = lax.dot_general(q.astype(BF16), k.astype(BF16), (((1,), (1,)), ((), ())),
                                preferred_element_type=F32)
        st = state_ref[hh]
        cross = jnp.dot((q * qd_ref[hh]).astype(BF16), st.astype(BF16), preferred_element_type=F32)
        kdt = jnp.transpose(k * kd_ref[hh]).astype(BF16)
        upd = jnp.dot(kdt, v, preferred_element_type=F32)
        yield
        state_ref[hh] = st * cd_ref[hh][0:1, :] + upd
        y = jnp.dot((inner * mask_ref[hh]).astype(BF16), v, preferred_element_type=F32) + cross
        yield
        mu = jnp.mean(y, axis=-1, keepdims=True)
        yc = y - mu
        var = jnp.mean(yc * yc, axis=-1, keepdims=True)
        yn = yc * lax.rsqrt(var + RET_GN_EPS) * gg_ref[:, cols] + gb_ref[:, cols]
        g = g_ref[rows, cols].astype(F32)
        o_ref[rows, cols] = (g * jax.nn.sigmoid(g) * yn).astype(o_ref.dtype)

    def body(c, carry):
        rows = pl.ds(pl.multiple_of(c * c_len, c_len), c_len)
        cos, sin = cos_ref[rows, :], sin_ref[rows, :]
        _round_robin([head_chunk(rows, hh, cos, sin) for hh in range(n_heads)])
        return carry

    lax.fori_loop(0, n_chunks, body, 0)


def _retention_core(qkvg, cos, sin, gn_g, gn_b, bn, t):
    d = qkvg.shape[1] // 4
    h = d // RET_HEAD_DIM
    c_len, dh = RET_CHUNK, RET_HEAD_DIM
    log_g = jnp.log1p(-(2.0 ** (-5.0 - jnp.arange(h, dtype=F32))))
    idx = jnp.arange(c_len, dtype=F32)
    diff = idx[:, None] - idx[None, :]
    mask = jnp.where(diff >= 0, jnp.exp(log_g[:, None, None] * jnp.maximum(diff, 0.0)), 0.0)
    q_decay = jnp.exp(log_g[:, None] * (idx + 1.0))
    k_decay = jnp.exp(log_g[:, None] * (c_len - 1.0 - idx))
    chunk_decay = jnp.exp(log_g * c_len)
    qd = jnp.broadcast_to(q_decay[:, :, None], (h, c_len, dh))
    kd = jnp.broadcast_to(k_decay[:, :, None], (h, c_len, dh))
    cd = jnp.broadcast_to(chunk_decay[:, None, None], (h, 8, dh))

    nh = min(RET_INTERLEAVE, h)
    bw = nh * dh
    rb = min(t, 1024)
    nr = t // rb
    hb = h // nh

    def col(off):
        return pl.BlockSpec((rb, bw), lambda b, j, i, off=off: (b * nr + i, off * hb + j))

    def per_head(rows, width):
        return pl.BlockSpec((nh, rows, width), lambda b, j, i: (j, 0, 0))

    tab = pl.BlockSpec((rb, dh // 2), lambda b, j, i: (b * nr + i, 0))
    vec = pl.BlockSpec((1, bw), lambda b, j, i: (0, j))
    return pl.pallas_call(
        functools.partial(_ret_kernel, n_chunks=rb // c_len, n_heads=nh),
        grid=(bn, hb, nr),
        in_specs=[col(0), col(1), col(2), col(3), tab, tab,
                  per_head(c_len, c_len), per_head(c_len, dh), per_head(c_len, dh), per_head(8, dh),
                  vec, vec],
        out_specs=pl.BlockSpec((rb, bw), lambda b, j, i: (b * nr + i, j)),
        out_shape=jax.ShapeDtypeStruct((bn * t, d), BF16),
        scratch_shapes=[pltpu.VMEM((nh, dh, dh), F32)],
        compiler_params=_cparams(("parallel", "parallel", "arbitrary"), VMEM_LIMIT),
        name="retention",
    )(qkvg, qkvg, qkvg, qkvg, cos, sin, mask, qd, kd, cd, gn_g.reshape(1, d), gn_b.reshape(1, d))


def _lora_act(t, act):
    if act == "tanh":
        return jnp.tanh(t)
    if act == "sigmoid":
        return jax.nn.sigmoid(t)
    return t


def _mix_kernel(x_ref, prev_ref, mu_ref, w1_ref, o_ref, t_ref, *, tiles_per_seq, loras):
    x = x_ref[...].astype(F32)
    first = (pl.program_id(0) % tiles_per_seq) == 0
    last = prev_ref.shape[0] - 1
    prev_row = jnp.where(first, 0.0, prev_ref[last:last + 1, :].astype(F32))
    shifted = pltpu.roll(x, 1, axis=0)
    rid = lax.broadcasted_iota(jnp.int32, x.shape, 0)
    xx = jnp.where(rid == 0, prev_row, shifted) - x
    for i in range(o_ref.shape[0]):
        o_ref[i] = (x + xx * mu_ref[i:i + 1, :]).astype(o_ref.dtype)
    for mix, act, lo, hi in loras:
        xm = (x + xx * mu_ref[mix:mix + 1, :]).astype(BF16)
        down = jnp.dot(xm, w1_ref[:, lo:hi], preferred_element_type=F32)
        t_ref[:, lo:hi] = _lora_act(down, act).astype(t_ref.dtype)


def _token_mix(x, mu, w1cat, loras, t):
    m, d = x.shape
    rt = w1cat.shape[1]
    tm = min(t, 256)
    pr = BF16_SUBLANES
    return pl.pallas_call(
        functools.partial(_mix_kernel, tiles_per_seq=t // tm, loras=loras),
        grid=(m // tm,),
        in_specs=[pl.BlockSpec((tm, d), lambda i: (i, 0)),
                  pl.BlockSpec((pr, d), lambda i: (jnp.maximum(i * (tm // pr) - 1, 0), 0)),
                  pl.BlockSpec((6, d), lambda i: (0, 0)),
                  pl.BlockSpec((d, rt), lambda i: (0, 0))],
        out_specs=[pl.BlockSpec((3, tm, d), lambda i: (0, i, 0)),
                   pl.BlockSpec((tm, rt), lambda i: (i, 0))],
        out_shape=[jax.ShapeDtypeStruct((3, m, d), BF16), jax.ShapeDtypeStruct((m, rt), BF16)],
        compiler_params=_cparams(("parallel",), VMEM_LIMIT),
        name="token_mix",
    )(x, x, mu, w1cat)


def _lora_up_kernel(t_ref, *refs, ranges):
    n = len(ranges)
    for (lo, hi), w2_ref, o_ref in zip(ranges, refs[:n], refs[n:]):
        o_ref[...] = jnp.dot(t_ref[:, lo:hi], w2_ref[...], preferred_element_type=F32).astype(o_ref.dtype)


def _pad_rank(r):
    return -(-r // LANES) * LANES


def _lora_prepare(specs):
    w1s, w2s, loras, col = [], [], [], 0
    for mix, w1, w2, act, _ in specs:
        r = w1.shape[1]
        rp = _pad_rank(r)
        w1s.append(jnp.pad(w1, ((0, 0), (0, rp - r))).astype(BF16))
        w2s.append(jnp.pad(w2, ((0, rp - r), (0, 0))).astype(BF16))
        loras.append((mix, act, col, col + rp))
        col += rp
    return jnp.concatenate(w1s, axis=1), w2s, tuple(loras)


def _lora_up(tact, w2s, loras, out_dtypes):
    m, rt = tact.shape
    d = w2s[0].shape[1]
    tm = min(m, 256)
    ranges = tuple((lo, hi) for _, _, lo, hi in loras)
    row = pl.BlockSpec((tm, d), lambda i: (i, 0))
    return pl.pallas_call(
        functools.partial(_lora_up_kernel, ranges=ranges),
        grid=(m // tm,),
        in_specs=[pl.BlockSpec((tm, rt), lambda i: (i, 0))]
                 + [pl.BlockSpec(w.shape, lambda i: (0, 0)) for w in w2s],
        out_specs=[row] * len(w2s),
        out_shape=[jax.ShapeDtypeStruct((m, d), dt) for dt in out_dtypes],
        compiler_params=_cparams(("parallel",), VMEM_LIMIT),
        name="lora_up",
    )(tact, *w2s)


def _split2(x):
    hi = x.astype(BF16)
    return hi, (x - hi.astype(F32)).astype(BF16)


def _block_diag(y, half_masks):
    yb = y.astype(BF16)
    zeros = jnp.zeros((y.shape[0], LANES), BF16)
    blocks = []
    for h in range(WKV_GROUP):
        col = h // 2
        part = yb[:, col * LANES:(col + 1) * LANES] * half_masks[h % 2]
        blocks.append(jnp.concatenate([part, zeros] if col == 0 else [zeros, part], axis=1))
    return jnp.concatenate(blocks, axis=0)


def _wkv_kernel(r_ref, k_ref, v_ref, lw_ref, la_ref, lg_ref, *rest, n_chunks, n_groups, vres):
    if vres:
        lv_ref, vf_ref = rest[0], rest[1]
        rest = rest[2:]
    (w0_ref, a0_ref, kk_ref, ka_ref, rk_ref, gg_ref, gb_ref, v0_ref,
     ones_ref, tri_ref, lo_ref, bdm_ref, o_ref, state_ref) = rest
    L, gw = WKV_CHUNK, MXU_DIM

    @pl.when(pl.program_id(2) == 0)
    def _():
        state_ref[...] = jnp.zeros_like(state_ref)

    lane = lax.broadcasted_iota(jnp.int32, (1, LANES), 1)
    half_masks = [(lane < RWKV_HEAD_DIM).astype(BF16), (lane >= RWKV_HEAD_DIM).astype(BF16)]

    def mm(a, b):
        return jnp.dot(a, b, preferred_element_type=F32)

    def seg_sum(x):
        return mm(x.astype(BF16), ones_ref[...])

    def bd(y):
        return _block_diag(y, half_masks)

    def group_chunk(rows, q):
        cols = slice(q * gw, (q + 1) * gw)
        r = r_ref[rows, cols].astype(F32)
        k0 = k_ref[rows, cols].astype(F32)
        v = v_ref[rows, cols].astype(F32)
        z = -(w0_ref[:, cols] + lw_ref[rows, cols])
        softplus = jnp.maximum(z, 0.0) + jnp.log1p(jnp.exp(-jnp.abs(z)))
        logw = -jnp.exp(-softplus - 0.5)
        a = jax.nn.sigmoid(a0_ref[:, cols] + la_ref[rows, cols].astype(F32))
        if vres:
            v = v + (vf_ref[rows, cols].astype(F32) - v) * jax.nn.sigmoid(
                v0_ref[:, cols] + lv_ref[rows, cols].astype(F32))
        kk = k0 * kk_ref[:, cols]
        ss = seg_sum(kk * kk)
        hi, lo = _split2(logw)
        cum = mm(tri_ref[...], hi) + mm(tri_ref[...], lo)
        yield
        kk = kk * lax.rsqrt(jnp.maximum(ss, 1e-24))
        k = k0 * (1.0 + (a - 1.0) * ka_ref[:, cols])
        avec, bvec = -kk, kk * a
        cum_l = cum[L - 1:L, :]
        e_out = jnp.exp(-cum)
        tail = jnp.exp(cum_l - cum)
        at = (avec * jnp.exp(cum - logw)).astype(BF16)
        rt = (r * jnp.exp(cum)).astype(BF16)
        ar = jnp.concatenate([at, rt], axis=0)
        bk = jnp.concatenate([bd(bvec * e_out), bd(k * e_out)], axis=0)
        p = lax.dot_general(ar, bk, (((1,), (1,)), ((), ())), preferred_element_type=F32)
        st = state_ref[q]
        ph = lax.dot_general(ar, st.astype(BF16), (((1,), (1,)), ((), ())),
                             preferred_element_type=F32)
        bonus_s = seg_sum(r * k * rk_ref[:, cols])
        yield
        strict, incl = lo_ref[0], lo_ref[1]
        m_ab = p[:L, :gw] * strict
        m_ak = p[:L, gw:] * strict
        n_ab = p[L:, :gw] * incl
        n_ak = p[L:, gw:] * incl
        pv = mm(jnp.concatenate([m_ak, n_ak], axis=0).astype(BF16), bd(v))
        x = m_ab
        x2 = mm(x.astype(BF16), bd(x))
        yield
        u = ph[:L] + pv[:L]
        for j in range(6):
            du = mm(x.astype(BF16), bd(u))
            if j < 5:
                x = x2
            if j < 4:
                x2 = mm(x.astype(BF16), bd(x))
            yield
            u = u + du

        yn_u = mm(n_ab.astype(BF16), bd(u))
        uv = jnp.concatenate([u, v], axis=0).astype(BF16)
        bkh = jnp.concatenate([bvec * tail, k * tail], axis=0).astype(BF16)
        upd = lax.dot_general(uv, bkh, (((0,), (0,)), ((), ())), preferred_element_type=F32)
        yield
        state_ref[q] = st * jnp.exp(cum_l) + upd * bdm_ref[...]
        y = ph[L:] + pv[L:] + yn_u
        mu = seg_sum(y) * (1.0 / RWKV_HEAD_DIM)
        yield
        yc = y - mu
        var = seg_sum(yc * yc) * (1.0 / RWKV_HEAD_DIM)
        yield
        yn = yc * lax.rsqrt(var + RWKV_GN_EPS) * gg_ref[:, cols] + gb_ref[:, cols]
        o_ref[rows, cols] = ((yn + bonus_s * v) * lg_ref[rows, cols].astype(F32)).astype(o_ref.dtype)

    def body(c, carry):
        rows = pl.ds(pl.multiple_of(c * L, L), L)
        _round_robin([group_chunk(rows, q) for q in range(n_groups)])
        return carry

    lax.fori_loop(0, n_chunks, body, 0)


def _wkv_core(rkv, lw, la, lg, lv, v_first, params, bn, t):
    _, m, d = rkv.shape
    L, gw = WKV_CHUNK, MXU_DIM
    n_groups = min(WKV_INTERLEAVE, d // gw)
    bw = n_groups * gw
    rb = min(t, 256)
    vres = lv is not None
    hid = jnp.arange(gw) // RWKV_HEAD_DIM
    bdm = (hid[:, None] == hid[None, :])
    ones_bd = bdm.astype(BF16)
    ti = jnp.arange(L)
    tri = (ti[:, None] >= ti[None, :]).astype(BF16)
    si = jnp.arange(gw) % L
    lo_masks = jnp.stack([(si[None, :] < ti[:, None]), (si[None, :] <= ti[:, None])]).astype(F32)

    def act(g=None):
        if g is None:
            return pl.BlockSpec((rb, bw), lambda b, q, i: (b * (t // rb) + i, q))
        return pl.BlockSpec((None, rb, bw), lambda b, q, i, g=g: (g, b * (t // rb) + i, q))

    vec = pl.BlockSpec((1, bw), lambda b, q, i: (0, q))

    def const(shape):
        return pl.BlockSpec(shape, lambda b, q, i: (0,) * len(shape))

    ins = [rkv, rkv, rkv, lw, la, lg]
    specs = [act(0), act(1), act(2), act(), act(), act()]
    if vres:
        ins += [lv, v_first]
        specs += [act(), act(2)]
    ins += [p.reshape(1, d) for p in params]
    specs += [vec] * len(params)
    ins += [ones_bd, tri, lo_masks, bdm.astype(F32)]
    specs += [const((gw, gw)), const((L, L)), const((2, L, gw)), const((gw, gw))]
    return pl.pallas_call(
        functools.partial(_wkv_kernel, n_chunks=rb // L, n_groups=n_groups, vres=vres),
        grid=(bn, d // bw, t // rb),
        in_specs=specs,
        out_specs=act(),
        out_shape=jax.ShapeDtypeStruct((m, d), BF16),
        scratch_shapes=[pltpu.VMEM((n_groups, gw, gw), F32)],
        compiler_params=_cparams(("parallel", "parallel", "arbitrary"), VMEM_LIMIT),
        name="wkv7",
    )(*ins)


def kernel(x, positions, ret_w_in, ret_gn_g, ret_gn_b, ret_w_o, rwkv_mu, rwkv_w_rkv, rwkv_w0, rwkv_w1, rwkv_w2, rwkv_a0, rwkv_a1, rwkv_a2, rwkv_g1, rwkv_g2, rwkv_k_k, rwkv_k_a, rwkv_r_k, rwkv_gn_g, rwkv_gn_b, rwkv_w_o, rwkv_v0, rwkv_v1, rwkv_v2, ln_mix_g, ln_mix_b, mlp_w1, mlp_w2, ln_mlp_g, ln_mlp_b):
    bn, t, d = x.shape
    m = bn * t
    depth = ln_mix_g.shape[0]
    alpha = (2 * depth) ** 0.25
    assert d % RET_HEAD_DIM == 0 and t % RET_CHUNK == 0 and t % WKV_CHUNK == 0

    xs = x.reshape(m, d).astype(F32)
    xb = xs.astype(BF16)
    cos, sin = _rope_tables(positions)
    v_first = None
    mix_order = jnp.array([0, 2, 3, 1, 4, 5])

    for i in range(depth):
        j = i // N_MIXERS
        if i % N_MIXERS == 0:
            qkvg = _mm2d(xb, ret_w_in, j)
            gated = _retention_core(qkvg, cos, sin, ret_gn_g[j], ret_gn_b[j], bn, t)
            h = _mm2d(gated, ret_w_o, j)
        else:
            specs = [(3, rwkv_w1[j], rwkv_w2[j], "tanh", F32),
                     (4, rwkv_a1[j], rwkv_a2[j], None, BF16),
                     (5, rwkv_g1[j], rwkv_g2[j], "sigmoid", BF16)]
            if j > 0:
                specs.append((2, rwkv_v1[j - 1], rwkv_v2[j - 1], None, BF16))
            w1cat, w2s, loras = _lora_prepare(specs)
            x3, tact = _token_mix(xb, rwkv_mu[j][mix_order], w1cat, loras, t)
            rkv = _matmul(x3, rwkv_w_rkv.reshape(-1, d, d), g=3, w_off=3 * j)
            ups = _lora_up(tact, w2s, loras, [s[4] for s in specs])
            lw, la, lg = ups[:3]
            if j == 0:
                lv, v0 = None, jnp.zeros((d,), F32)
                v_first = rkv
            else:
                lv, v0 = ups[3], rwkv_v0[j - 1]
            params = (rwkv_w0[j], rwkv_a0[j], rwkv_k_k[j], rwkv_k_a[j], rwkv_r_k[j].reshape(d),
                      rwkv_gn_g[j], rwkv_gn_b[j], v0)
            gated = _wkv_core(rkv, lw, la, lg, lv, v_first, params, bn, t)
            h = _mm2d(gated, rwkv_w_o, j)
        xs = xb = _add_ln(xs, h, ln_mix_g[i], ln_mix_b[i], alpha, BF16)
        hid = _mm2d(xb, mlp_w1, i, act="relu2")
        h = _matmul_long_k(hid, mlp_w2[i].astype(BF16))
        xs = xb = _add_ln(xs, h, ln_mlp_g[i], ln_mlp_b[i], alpha, BF16 if i + 1 < depth else F32)
    return xs.reshape(bn, t, d).astype(x.dtype)
```

```python
import functools

import jax
import jax.numpy as jnp
from jax import lax
from jax.experimental import pallas as pl
from jax.experimental.pallas import tpu as pltpu

F32 = jnp.float32
BF16 = jnp.bfloat16

RET_HEAD_DIM = 256
RET_CHUNK = 128
ROPE_BASE = 10000.0
RET_GN_EPS = 1e-5
RWKV_HEAD_DIM = 64
RWKV_GN_EPS = 64e-5
LN_EPS = 1e-5
N_MIXERS = 2

LANES = 128
BF16_SUBLANES = 16
MXU_DIM = 256
WKV_CHUNK = 64
WKV_GROUP = MXU_DIM // RWKV_HEAD_DIM
WKV_INTERLEAVE = 8
RET_INTERLEAVE = 4
VMEM_LIMIT = 56 * 1024 * 1024


def _cparams(sem, vmem=None):
    return pltpu.CompilerParams(dimension_semantics=sem, vmem_limit_bytes=vmem)


def _round_robin(stage_generators):
    live = list(stage_generators)
    while live:
        nxt = []
        for g in live:
            try:
                next(g)
                nxt.append(g)
            except StopIteration:
                pass
        live = nxt


def _act(x, act):
    if act == "relu2":
        return jnp.square(jnp.maximum(x, 0.0))
    return x


def _mm_kernel(x_ref, w_ref, o_ref, *, act):
    part = jnp.dot(x_ref[...], w_ref[...].astype(BF16), preferred_element_type=F32)
    o_ref[...] = _act(part, act).astype(o_ref.dtype)


def _mm_kt_kernel(x_ref, w_ref, o_ref, acc_ref, *, nk, act):
    part = jnp.dot(x_ref[...], w_ref[...], preferred_element_type=F32)
    k = pl.program_id(2)

    @pl.when(k == 0)
    def _():
        acc_ref[...] = part

    @pl.when(k > 0)
    def _():
        acc_ref[...] += part

    @pl.when(k == nk - 1)
    def _():
        o_ref[...] = _act(acc_ref[...], act).astype(o_ref.dtype)


def _matmul(x, w, *, g=1, w_off=0, act=None, out_dtype=BF16):
    _, m, kd = x.shape
    _, _, n = w.shape
    tm, tn = min(m, 1024), min(n, 512)
    return pl.pallas_call(
        functools.partial(_mm_kernel, act=act),
        grid=(g, m // tm, n // tn),
        in_specs=[pl.BlockSpec((None, tm, kd), lambda b, i, j: (b, i, 0)),
                  pl.BlockSpec((None, kd, tn), lambda b, i, j: (w_off + b, 0, j))],
        out_specs=pl.BlockSpec((None, tm, tn), lambda b, i, j: (b, i, j)),
        out_shape=jax.ShapeDtypeStruct((g, m, n), out_dtype),
        compiler_params=_cparams(("parallel", "parallel", "parallel"), VMEM_LIMIT),
        name="matmul",
    )(x, w)


def _matmul_long_k(x, w_stack, layer, *, act=None, out_dtype=BF16):
    m, kd = x.shape
    n = w_stack.shape[2]
    tm, tn, tk = min(m, 1024), min(n, 1024), min(kd, 4096)
    nk = kd // tk
    return pl.pallas_call(
        functools.partial(_mm_kt_kernel, nk=nk, act=act),
        grid=(m // tm, n // tn, nk),
        in_specs=[pl.BlockSpec((tm, tk), lambda i, j, k: (i, k)),
                  pl.BlockSpec((None, tk, tn), lambda i, j, k: (layer, k, j))],
        out_specs=pl.BlockSpec((tm, tn), lambda i, j, k: (i, j)),
        out_shape=jax.ShapeDtypeStruct((m, n), out_dtype),
        scratch_shapes=[pltpu.VMEM((tm, tn), F32)],
        compiler_params=_cparams(("parallel", "parallel", "arbitrary"), VMEM_LIMIT),
        name="matmul_long_k",
    )(x, w_stack)


def _mm2d(x, w_stack, layer, **kw):
    return _matmul(x[None], w_stack, w_off=layer, **kw)[0]


def _add_ln_kernel(x_ref, h_ref, g_ref, b_ref, o_ref, *, alpha):
    z = alpha * x_ref[...].astype(F32) + h_ref[...].astype(F32)
    mu = jnp.mean(z, axis=-1, keepdims=True)
    zc = z - mu
    var = jnp.mean(zc * zc, axis=-1, keepdims=True)
    y = zc * lax.rsqrt(var + LN_EPS) * g_ref[...] + b_ref[...]
    o_ref[...] = y.astype(o_ref.dtype)


def _add_ln(x, h, g, b, alpha, out_dtype):
    m, d = x.shape
    tm = min(m, 256)
    row = pl.BlockSpec((tm, d), lambda i: (i, 0))
    vec = pl.BlockSpec((1, d), lambda i: (0, 0))
    return pl.pallas_call(
        functools.partial(_add_ln_kernel, alpha=alpha),
        grid=(m // tm,),
        in_specs=[row, row, vec, vec],
        out_specs=row,
        out_shape=jax.ShapeDtypeStruct((m, d), out_dtype),
        compiler_params=_cparams(("parallel",), VMEM_LIMIT),
        name="add_ln",
    )(x, h, g.reshape(1, d), b.reshape(1, d))


def _rope_kernel(pos_ref, freq_ref, cos_ref, sin_ref):
    ang = pos_ref[...].astype(F32) * freq_ref[...]
    cos_ref[...] = jnp.cos(ang)
    sin_ref[...] = jnp.sin(ang)


def _rope_tables(positions):
    m = positions.size
    half = RET_HEAD_DIM // 2
    inv_freq = ROPE_BASE ** (-jnp.arange(half, dtype=F32) / half)
    tm = min(m, 512)
    out = pl.BlockSpec((tm, half), lambda i: (i, 0))
    return pl.pallas_call(
        _rope_kernel,
        grid=(m // tm,),
        in_specs=[pl.BlockSpec((tm, 1), lambda i: (i, 0)),
                  pl.BlockSpec((1, half), lambda i: (0, 0))],
        out_specs=[out, out],
        out_shape=[jax.ShapeDtypeStruct((m, half), F32)] * 2,
        compiler_params=_cparams(("parallel",)),
        name="rope_tables",
    )(positions.reshape(m, 1), inv_freq.reshape(1, half))


def _ret_kernel(q_ref, k_ref, v_ref, g_ref, cos_ref, sin_ref, mask_ref, qd_ref, kd_ref, cd_ref,
                gg_ref, gb_ref, o_ref, state_ref, *, n_chunks, n_heads):
    c_len, dh = RET_CHUNK, RET_HEAD_DIM
    half = dh // 2

    @pl.when(pl.program_id(2) == 0)
    def _():
        state_ref[...] = jnp.zeros_like(state_ref)

    def rot(t, cos, sin):
        t1, t2 = t[:, :half], t[:, half:]
        return jnp.concatenate([t1 * cos - t2 * sin, t1 * sin + t2 * cos], axis=-1)

    def head_chunk(rows, hh, cos, sin):
        cols = slice(hh * dh, (hh + 1) * dh)
        q = rot(q_ref[rows, cols].astype(F32), cos, sin)
        k = rot(k_ref[rows, cols].astype(F32), cos, sin) * (dh ** -0.5)
        v = v_ref[rows, cols]
        inner = lax.dot_general(q.astype(BF16), k.astype(BF16), (((1,), (1,)), ((), ())),
                                preferred_element_type=F32)
        st = state_ref[hh]
        cross = jnp.dot((q * qd_ref[hh]).astype(BF16), st.astype(BF16), preferred_element_type=F32)
        kdt = jnp.transpose(k * kd_ref[hh]).astype(BF16)
        upd = jnp.dot(kdt, v, preferred_element_type=F32)
        yield
        state_ref[hh] = st * cd_ref[hh][0:1, :] + upd
        y = jnp.dot((inner * mask_ref[hh]).astype(BF16), v, preferred_element_type=F32) + cross
        yield
        mu = jnp.mean(y, axis=-1, keepdims=True)
        yc = y - mu
        var = jnp.mean(yc * yc, axis=-1, keepdims=True)
        yn = yc * lax.rsqrt(var + RET_GN_EPS) * gg_ref[:, cols] + gb_ref[:, cols]
        g = g_ref[rows, cols].astype(F32)
        o_ref[rows, cols] = (g * jax.nn.sigmoid(g) * yn).astype(o_ref.dtype)

    def body(c, carry):
        rows = pl.ds(pl.multiple_of(c * c_len, c_len), c_len)
        cos, sin = cos_ref[rows, :], sin_ref[rows, :]
        _round_robin([head_chunk(rows, hh, cos, sin) for hh in range(n_heads)])
        return carry

    lax.fori_loop(0, n_chunks, body, 0)


def _retention_core(qkvg, cos, sin, gn_g, gn_b, bn, t):
    d = qkvg.shape[1] // 4
    h = d // RET_HEAD_DIM
    c_len, dh = RET_CHUNK, RET_HEAD_DIM
    log_g = jnp.log1p(-(2.0 ** (-5.0 - jnp.arange(h, dtype=F32))))
    idx = jnp.arange(c_len, dtype=F32)
    diff = idx[:, None] - idx[None, :]
    mask = jnp.where(diff >= 0, jnp.exp(log_g[:, None, None] * jnp.maximum(diff, 0.0)), 0.0)
    q_decay = jnp.exp(log_g[:, None] * (idx + 1.0))
    k_decay = jnp.exp(log_g[:, None] * (c_len - 1.0 - idx))
    chunk_decay = jnp.exp(log_g * c_len)
    qd = jnp.broadcast_to(q_decay[:, :, None], (h, c_len, dh))
    kd = jnp.broadcast_to(k_decay[:, :, None], (h, c_len, dh))
    cd = jnp.broadcast_to(chunk_decay[:, None, None], (h, 8, dh))

    nh = min(RET_INTERLEAVE, h)
    bw = nh * dh
    rb = min(t, 1024)
    nr = t // rb
    hb = h // nh

    def col(off):
        return pl.BlockSpec((rb, bw), lambda b, j, i, off=off: (b * nr + i, off * hb + j))

    def per_head(rows, width):
        return pl.BlockSpec((nh, rows, width), lambda b, j, i: (j, 0, 0))

    tab = pl.BlockSpec((rb, dh // 2), lambda b, j, i: (b * nr + i, 0))
    vec = pl.BlockSpec((1, bw), lambda b, j, i: (0, j))
    return pl.pallas_call(
        functools.partial(_ret_kernel, n_chunks=rb // c_len, n_heads=nh),
        grid=(bn, hb, nr),
        in_specs=[col(0), col(1), col(2), col(3), tab, tab,
                  per_head(c_len, c_len), per_head(c_len, dh), per_head(c_len, dh), per_head(8, dh),
                  vec, vec],
        out_specs=pl.BlockSpec((rb, bw), lambda b, j, i: (b * nr + i, j)),
        out_shape=jax.ShapeDtypeStruct((bn * t, d), BF16),
        scratch_shapes=[pltpu.VMEM((nh, dh, dh), F32)],
        compiler_params=_cparams(("parallel", "parallel", "arbitrary"), VMEM_LIMIT),
        name="retention",
    )(qkvg, qkvg, qkvg, qkvg, cos, sin, mask, qd, kd, cd, gn_g.reshape(1, d), gn_b.reshape(1, d))


def _lora_act(t, act):
    if act == "tanh":
        return jnp.tanh(t)
    if act == "sigmoid":
        return jax.nn.sigmoid(t)
    return t


def _mix_kernel(x_ref, prev_ref, mu_ref, w1_ref, o_ref, t_ref, *, tiles_per_seq, loras):
    x = x_ref[...].astype(F32)
    first = (pl.program_id(0) % tiles_per_seq) == 0
    last = prev_ref.shape[0] - 1
    prev_row = jnp.where(first, 0.0, prev_ref[last:last + 1, :].astype(F32))
    shifted = pltpu.roll(x, 1, axis=0)
    rid = lax.broadcasted_iota(jnp.int32, x.shape, 0)
    xx = jnp.where(rid == 0, prev_row, shifted) - x
    for i in range(o_ref.shape[0]):
        o_ref[i] = (x + xx * mu_ref[i:i + 1, :]).astype(o_ref.dtype)
    for mix, act, lo, hi in loras:
        xm = (x + xx * mu_ref[mix:mix + 1, :]).astype(BF16)
        down = jnp.dot(xm, w1_ref[:, lo:hi], preferred_element_type=F32)
        t_ref[:, lo:hi] = _lora_act(down, act).astype(t_ref.dtype)


def _token_mix(x, mu, w1cat, loras, t):
    m, d = x.shape
    rt = w1cat.shape[1]
    tm = min(t, 256)
    pr = BF16_SUBLANES
    return pl.pallas_call(
        functools.partial(_mix_kernel, tiles_per_seq=t // tm, loras=loras),
        grid=(m // tm,),
        in_specs=[pl.BlockSpec((tm, d), lambda i: (i, 0)),
                  pl.BlockSpec((pr, d), lambda i: (jnp.maximum(i * (tm // pr) - 1, 0), 0)),
                  pl.BlockSpec((6, d), lambda i: (0, 0)),
                  pl.BlockSpec((d, rt), lambda i: (0, 0))],
        out_specs=[pl.BlockSpec((3, tm, d), lambda i: (0, i, 0)),
                   pl.BlockSpec((tm, rt), lambda i: (i, 0))],
        out_shape=[jax.ShapeDtypeStruct((3, m, d), BF16), jax.ShapeDtypeStruct((m, rt), BF16)],
        compiler_params=_cparams(("parallel",), VMEM_LIMIT),
        name="token_mix",
    )(x, x, mu, w1cat)


def _lora_up_kernel(t_ref, *refs, ranges):
    n = len(ranges)
    for (lo, hi), w2_ref, o_ref in zip(ranges, refs[:n], refs[n:]):
        o_ref[...] = jnp.dot(t_ref[:, lo:hi], w2_ref[...], preferred_element_type=F32).astype(o_ref.dtype)


def _pad_rank(r):
    return -(-r // LANES) * LANES


def _lora_prepare(specs):
    w1s, w2s, loras, col = [], [], [], 0
    for mix, w1, w2, act, _ in specs:
        r = w1.shape[1]
        rp = _pad_rank(r)
        w1s.append(jnp.pad(w1, ((0, 0), (0, rp - r))).astype(BF16))
        w2s.append(jnp.pad(w2, ((0, rp - r), (0, 0))).astype(BF16))
        loras.append((mix, act, col, col + rp))
        col += rp
    return jnp.concatenate(w1s, axis=1), w2s, tuple(loras)


def _lora_up(tact, w2s, loras, out_dtypes):
    m, rt = tact.shape
    d = w2s[0].shape[1]
    tm = min(m, 256)
    ranges = tuple((lo, hi) for _, _, lo, hi in loras)
    row = pl.BlockSpec((tm, d), lambda i: (i, 0))
    return pl.pallas_call(
        functools.partial(_lora_up_kernel, ranges=ranges),
        grid=(m // tm,),
        in_specs=[pl.BlockSpec((tm, rt), lambda i: (i, 0))]
                 + [pl.BlockSpec(w.shape, lambda i: (0, 0)) for w in w2s],
        out_specs=[row] * len(w2s),
        out_shape=[jax.ShapeDtypeStruct((m, d), dt) for dt in out_dtypes],
        compiler_params=_cparams(("parallel",), VMEM_LIMIT),
        name="lora_up",
    )(tact, *w2s)


def _split2(x):
    hi = x.astype(BF16)
    return hi, (x - hi.astype(F32)).astype(BF16)


def _block_diag(y, half_masks):
    yb = y.astype(BF16)
    zeros = jnp.zeros((y.shape[0], LANES), BF16)
    blocks = []
    for h in range(WKV_GROUP):
        col = h // 2
        part = yb[:, col * LANES:(col + 1) * LANES] * half_masks[h % 2]
        blocks.append(jnp.concatenate([part, zeros] if col == 0 else [zeros, part], axis=1))
    return jnp.concatenate(blocks, axis=0)


def _wkv_kernel(r_ref, k_ref, v_ref, lw_ref, la_ref, lg_ref, *rest, n_chunks, n_groups, vres):
    if vres:
        lv_ref, vf_ref = rest[0], rest[1]
        rest = rest[2:]
    (w0_ref, a0_ref, kk_ref, ka_ref, rk_ref, gg_ref, gb_ref, v0_ref,
     ones_ref, tri_ref, lo_ref, bdm_ref, o_ref, state_ref) = rest
    L, gw = WKV_CHUNK, MXU_DIM

    @pl.when(pl.program_id(2) == 0)
    def _():
        state_ref[...] = jnp.zeros_like(state_ref)

    lane = lax.broadcasted_iota(jnp.int32, (1, LANES), 1)
    half_masks = [(lane < RWKV_HEAD_DIM).astype(BF16), (lane >= RWKV_HEAD_DIM).astype(BF16)]

    def mm(a, b):
        return jnp.dot(a, b, preferred_element_type=F32)

    def seg_sum(x):
        return mm(x.astype(BF16), ones_ref[...])

    def bd(y):
        return _block_diag(y, half_masks)

    def group_chunk(rows, q):
        cols = slice(q * gw, (q + 1) * gw)
        r = r_ref[rows, cols].astype(F32)
        k0 = k_ref[rows, cols].astype(F32)
        v = v_ref[rows, cols].astype(F32)
        z = -(w0_ref[:, cols] + lw_ref[rows, cols])
        softplus = jnp.maximum(z, 0.0) + jnp.log1p(jnp.exp(-jnp.abs(z)))
        logw = -jnp.exp(-softplus - 0.5)
        a = jax.nn.sigmoid(a0_ref[:, cols] + la_ref[rows, cols].astype(F32))
        if vres:
            v = v + (vf_ref[rows, cols].astype(F32) - v) * jax.nn.sigmoid(
                v0_ref[:, cols] + lv_ref[rows, cols].astype(F32))
        kk = k0 * kk_ref[:, cols]
        ss = seg_sum(kk * kk)
        hi, lo = _split2(logw)
        cum = mm(tri_ref[...], hi) + mm(tri_ref[...], lo)
        yield
        kk = kk * lax.rsqrt(jnp.maximum(ss, 1e-24))
        k = k0 * (1.0 + (a - 1.0) * ka_ref[:, cols])
        avec, bvec = -kk, kk * a
        cum_l = cum[L - 1:L, :]
        e_out = jnp.exp(-cum)
        tail = jnp.exp(cum_l - cum)
        at = (avec * jnp.exp(cum - logw)).astype(BF16)
        rt = (r * jnp.exp(cum)).astype(BF16)
        ar = jnp.concatenate([at, rt], axis=0)
        bk = jnp.concatenate([bd(bvec * e_out), bd(k * e_out)], axis=0)
        p = lax.dot_general(ar, bk, (((1,), (1,)), ((), ())), preferred_element_type=F32)
        st = state_ref[q]
        ph = lax.dot_general(ar, st.astype(BF16), (((1,), (1,)), ((), ())),
                             preferred_element_type=F32)
        bonus_s = seg_sum(r * k * rk_ref[:, cols])
        yield
        strict, incl = lo_ref[0], lo_ref[1]
        m_ab = p[:L, :gw] * strict
        m_ak = p[:L, gw:] * strict
        n_ab = p[L:, :gw] * incl
        n_ak = p[L:, gw:] * incl
        pv = mm(jnp.concatenate([m_ak, n_ak], axis=0).astype(BF16), bd(v))
        x = m_ab
        x2 = mm(x.astype(BF16), bd(x))
        yield
        u = ph[:L] + pv[:L]
        for j in range(6):
            du = mm(x.astype(BF16), bd(u))
            if j < 5:
                x = x2
            if j < 4:
                x2 = mm(x.astype(BF16), bd(x))
            yield
            u = u + du

        yn_u = mm(n_ab.astype(BF16), bd(u))
        uv = jnp.concatenate([u, v], axis=0).astype(BF16)
        bkh = jnp.concatenate([bvec * tail, k * tail], axis=0).astype(BF16)
        upd = lax.dot_general(uv, bkh, (((0,), (0,)), ((), ())), preferred_element_type=F32)
        yield
        state_ref[q] = st * jnp.exp(cum_l) + upd * bdm_ref[...]
        y = ph[L:] + pv[L:] + yn_u
        mu = seg_sum(y) * (1.0 / RWKV_HEAD_DIM)
        yield
        yc = y - mu
        var = seg_sum(yc * yc) * (1.0 / RWKV_HEAD_DIM)
        yield
        yn = yc * lax.rsqrt(var + RWKV_GN_EPS) * gg_ref[:, cols] + gb_ref[:, cols]
        o_ref[rows, cols] = ((yn + bonus_s * v) * lg_ref[rows, cols].astype(F32)).astype(o_ref.dtype)

    def body(c, carry):
        rows = pl.ds(pl.multiple_of(c * L, L), L)
        _round_robin([group_chunk(rows, q) for q in range(n_groups)])
        return carry

    lax.fori_loop(0, n_chunks, body, 0)


def _wkv_core(rkv, lw, la, lg, lv, v_first, params, bn, t):
    _, m, d = rkv.shape
    L, gw = WKV_CHUNK, MXU_DIM
    n_groups = min(WKV_INTERLEAVE, d // gw)
    bw = n_groups * gw
    rb = min(t, 256)
    vres = lv is not None
    hid = jnp.arange(gw) // RWKV_HEAD_DIM
    bdm = (hid[:, None] == hid[None, :])
    ones_bd = bdm.astype(BF16)
    ti = jnp.arange(L)
    tri = (ti[:, None] >= ti[None, :]).astype(BF16)
    si = jnp.arange(gw) % L
    lo_masks = jnp.stack([(si[None, :] < ti[:, None]), (si[None, :] <= ti[:, None])]).astype(F32)

    def act(g=None):
        if g is None:
            return pl.BlockSpec((rb, bw), lambda b, q, i: (b * (t // rb) + i, q))
        return pl.BlockSpec((None, rb, bw), lambda b, q, i, g=g: (g, b * (t // rb) + i, q))

    vec = pl.BlockSpec((1, bw), lambda b, q, i: (0, q))

    def const(shape):
        return pl.BlockSpec(shape, lambda b, q, i: (0,) * len(shape))

    ins = [rkv, rkv, rkv, lw, la, lg]
    specs = [act(0), act(1), act(2), act(), act(), act()]
    if vres:
        ins += [lv, v_first]
        specs += [act(), act(2)]
    ins += [p.reshape(1, d) for p in params]
    specs += [vec] * len(params)
    ins += [ones_bd, tri, lo_masks, bdm.astype(F32)]
    specs += [const((gw, gw)), const((L, L)), const((2, L, gw)), const((gw, gw))]
    return pl.pallas_call(
        functools.partial(_wkv_kernel, n_chunks=rb // L, n_groups=n_groups, vres=vres),
        grid=(bn, d // bw, t // rb),
        in_specs=specs,
        out_specs=act(),
        out_shape=jax.ShapeDtypeStruct((m, d), BF16),
        scratch_shapes=[pltpu.VMEM((n_groups, gw, gw), F32)],
        compiler_params=_cparams(("parallel", "parallel", "arbitrary"), VMEM_LIMIT),
        name="wkv7",
    )(*ins)


def kernel(x, positions, ret_w_in, ret_gn_g, ret_gn_b, ret_w_o, rwkv_mu, rwkv_w_rkv, rwkv_w0, rwkv_w1, rwkv_w2, rwkv_a0, rwkv_a1, rwkv_a2, rwkv_g1, rwkv_g2, rwkv_k_k, rwkv_k_a, rwkv_r_k, rwkv_gn_g, rwkv_gn_b, rwkv_w_o, rwkv_v0, rwkv_v1, rwkv_v2, ln_mix_g, ln_mix_b, mlp_w1, mlp_w2, ln_mlp_g, ln_mlp_b):
    bn, t, d = x.shape
    m = bn * t
    depth = ln_mix_g.shape[0]
    alpha = (2 * depth) ** 0.25
    assert d % RET_HEAD_DIM == 0 and t % RET_CHUNK == 0 and t % WKV_CHUNK == 0

    xs = x.reshape(m, d).astype(F32)
    xb = xs.astype(BF16)
    cos, sin = _rope_tables(positions)
    mlp_w2_b = mlp_w2.astype(BF16)
    v_first = None
    mix_order = jnp.array([0, 2, 3, 1, 4, 5])

    for i in range(depth):
        j = i // N_MIXERS
        if i % N_MIXERS == 0:
            qkvg = _mm2d(xb, ret_w_in, j)
            gated = _retention_core(qkvg, cos, sin, ret_gn_g[j], ret_gn_b[j], bn, t)
            h = _mm2d(gated, ret_w_o, j)
        else:
            specs = [(3, rwkv_w1[j], rwkv_w2[j], "tanh", F32),
                     (4, rwkv_a1[j], rwkv_a2[j], None, BF16),
                     (5, rwkv_g1[j], rwkv_g2[j], "sigmoid", BF16)]
            if j > 0:
                specs.append((2, rwkv_v1[j - 1], rwkv_v2[j - 1], None, BF16))
            w1cat, w2s, loras = _lora_prepare(specs)
            x3, tact = _token_mix(xb, rwkv_mu[j][mix_order], w1cat, loras, t)
            rkv = _matmul(x3, rwkv_w_rkv.reshape(-1, d, d), g=3, w_off=3 * j)
            ups = _lora_up(tact, w2s, loras, [s[4] for s in specs])
            lw, la, lg = ups[:3]
            if j == 0:
                lv, v0 = None, jnp.zeros((d,), F32)
                v_first = rkv
            else:
                lv, v0 = ups[3], rwkv_v0[j - 1]
            params = (rwkv_w0[j], rwkv_a0[j], rwkv_k_k[j], rwkv_k_a[j], rwkv_r_k[j].reshape(d),
                      rwkv_gn_g[j], rwkv_gn_b[j], v0)
            gated = _wkv_core(rkv, lw, la, lg, lv, v_first, params, bn, t)
            h = _mm2d(gated, rwkv_w_o, j)
        xs = xb = _add_ln(xs, h, ln_mix_g[i], ln_mix_b[i], alpha, BF16)
        hid = _mm2d(xb, mlp_w1, i, act="relu2")
        h = _matmul_long_k(hid, mlp_w2_b, i)
        xs = xb = _add_ln(xs, h, ln_mlp_g[i], ln_mlp_b[i], alpha, BF16 if i + 1 < depth else F32)
    return xs.reshape(bn, t, d).astype(x.dtype)
```

```python
import functools

import jax
import jax.numpy as jnp
from jax import lax
from jax.experimental import pallas as pl
from jax.experimental.pallas import tpu as pltpu

F32 = jnp.float32
BF16 = jnp.bfloat16

RET_HEAD_DIM = 256
RET_CHUNK = 128
ROPE_BASE = 10000.0
RET_GN_EPS = 1e-5
RWKV_HEAD_DIM = 64
RWKV_GN_EPS = 64e-5
LN_EPS = 1e-5
N_MIXERS = 2

LANES = 128
BF16_SUBLANES = 16
MXU_DIM = 256
WKV_CHUNK = 64
WKV_GROUP = MXU_DIM // RWKV_HEAD_DIM
WKV_INTERLEAVE = 8
RET_INTERLEAVE = 4
VMEM_LIMIT = 56 * 1024 * 1024


def _cparams(sem, vmem=None):
    return pltpu.CompilerParams(dimension_semantics=sem, vmem_limit_bytes=vmem)


def _round_robin(stage_generators):
    live = list(stage_generators)
    while live:
        nxt = []
        for g in live:
            try:
                next(g)
                nxt.append(g)
            except StopIteration:
                pass
        live = nxt


def _act(x, act):
    if act == "relu2":
        return jnp.square(jnp.maximum(x, 0.0))
    return x


def _mm_kernel(x_ref, w_ref, o_ref, *, act):
    part = jnp.dot(x_ref[...], w_ref[...].astype(BF16), preferred_element_type=F32)
    o_ref[...] = _act(part, act).astype(o_ref.dtype)


def _mm_kt_kernel(x_ref, w_ref, o_ref, acc_ref, *, nk, act):
    part = jnp.dot(x_ref[...], w_ref[...], preferred_element_type=F32)
    k = pl.program_id(2)

    @pl.when(k == 0)
    def _():
        acc_ref[...] = part

    @pl.when(k > 0)
    def _():
        acc_ref[...] += part

    @pl.when(k == nk - 1)
    def _():
        o_ref[...] = _act(acc_ref[...], act).astype(o_ref.dtype)


def _matmul(x, w, *, g=1, w_off=0, act=None, out_dtype=BF16):
    _, m, kd = x.shape
    _, _, n = w.shape
    tm, tn = min(m, 1024), min(n, 512)
    return pl.pallas_call(
        functools.partial(_mm_kernel, act=act),
        grid=(g, m // tm, n // tn),
        in_specs=[pl.BlockSpec((None, tm, kd), lambda b, i, j: (b, i, 0)),
                  pl.BlockSpec((None, kd, tn), lambda b, i, j: (w_off + b, 0, j))],
        out_specs=pl.BlockSpec((None, tm, tn), lambda b, i, j: (b, i, j)),
        out_shape=jax.ShapeDtypeStruct((g, m, n), out_dtype),
        compiler_params=_cparams(("parallel", "parallel", "parallel"), VMEM_LIMIT),
        name="matmul",
    )(x, w)


def _matmul_long_k(x, w_stack, layer, *, act=None, out_dtype=BF16):
    m, kd = x.shape
    n = w_stack.shape[2]
    tm, tn, tk = min(m, 1024), min(n, 1024), min(kd, 4096)
    nk = kd // tk
    return pl.pallas_call(
        functools.partial(_mm_kt_kernel, nk=nk, act=act),
        grid=(m // tm, n // tn, nk),
        in_specs=[pl.BlockSpec((tm, tk), lambda i, j, k: (i, k)),
                  pl.BlockSpec((None, tk, tn), lambda i, j, k: (layer, k, j))],
        out_specs=pl.BlockSpec((tm, tn), lambda i, j, k: (i, j)),
        out_shape=jax.ShapeDtypeStruct((m, n), out_dtype),
        scratch_shapes=[pltpu.VMEM((tm, tn), F32)],
        compiler_params=_cparams(("parallel", "parallel", "arbitrary"), VMEM_LIMIT),
        name="matmul_long_k",
    )(x, w_stack)


def _mm2d(x, w_stack, layer, **kw):
    return _matmul(x[None], w_stack, w_off=layer, **kw)[0]


def _add_ln_kernel(x_ref, h_ref, g_ref, b_ref, o_ref, *, alpha):
    z = alpha * x_ref[...].astype(F32) + h_ref[...].astype(F32)
    mu = jnp.mean(z, axis=-1, keepdims=True)
    zc = z - mu
    var = jnp.mean(zc * zc, axis=-1, keepdims=True)
    y = zc * lax.rsqrt(var + LN_EPS) * g_ref[...] + b_ref[...]
    o_ref[...] = y.astype(o_ref.dtype)


def _add_ln(x, h, g, b, alpha, out_dtype):
    m, d = x.shape
    tm = min(m, 256)
    row = pl.BlockSpec((tm, d), lambda i: (i, 0))
    vec = pl.BlockSpec((1, d), lambda i: (0, 0))
    return pl.pallas_call(
        functools.partial(_add_ln_kernel, alpha=alpha),
        grid=(m // tm,),
        in_specs=[row, row, vec, vec],
        out_specs=row,
        out_shape=jax.ShapeDtypeStruct((m, d), out_dtype),
        compiler_params=_cparams(("parallel",), VMEM_LIMIT),
        name="add_ln",
    )(x, h, g.reshape(1, d), b.reshape(1, d))


def _rope_kernel(pos_ref, freq_ref, cos_ref, sin_ref):
    ang = pos_ref[...].astype(F32) * freq_ref[...]
    cos_ref[...] = jnp.cos(ang)
    sin_ref[...] = jnp.sin(ang)


def _rope_tables(positions):
    m = positions.size
    half = RET_HEAD_DIM // 2
    inv_freq = ROPE_BASE ** (-jnp.arange(half, dtype=F32) / half)
    tm = min(m, 512)
    out = pl.BlockSpec((tm, half), lambda i: (i, 0))
    return pl.pallas_call(
        _rope_kernel,
        grid=(m // tm,),
        in_specs=[pl.BlockSpec((tm, 1), lambda i: (i, 0)),
                  pl.BlockSpec((1, half), lambda i: (0, 0))],
        out_specs=[out, out],
        out_shape=[jax.ShapeDtypeStruct((m, half), F32)] * 2,
        compiler_params=_cparams(("parallel",)),
        name="rope_tables",
    )(positions.reshape(m, 1), inv_freq.reshape(1, half))


def _ret_kernel(q_ref, k_ref, v_ref, g_ref, cos_ref, sin_ref, mask_ref, qd_ref, kd_ref, cd_ref,
                gg_ref, gb_ref, o_ref, state_ref, *, n_chunks, n_heads):
    c_len, dh = RET_CHUNK, RET_HEAD_DIM
    half = dh // 2

    @pl.when(pl.program_id(2) == 0)
    def _():
        state_ref[...] = jnp.zeros_like(state_ref)

    def rot(t, cos, sin):
        t1, t2 = t[:, :half], t[:, half:]
        return jnp.concatenate([t1 * cos - t2 * sin, t1 * sin + t2 * cos], axis=-1)

    def head_chunk(rows, hh, cos, sin):
        cols = slice(hh * dh, (hh + 1) * dh)
        q = rot(q_ref[rows, cols].astype(F32), cos, sin)
        k = rot(k_ref[rows, cols].astype(F32), cos, sin) * (dh ** -0.5)
        v = v_ref[rows, cols]
        inner = lax.dot_general(q.astype(BF16), k.astype(BF16), (((1,), (1,)), ((), ())),
                                preferred_element_type=F32)
        st = state_ref[hh]
        cross = jnp.dot((q * qd_ref[hh]).astype(BF16), st.astype(BF16), preferred_element_type=F32)
        kdt = jnp.transpose(k * kd_ref[hh]).astype(BF16)
        upd = jnp.dot(kdt, v, preferred_element_type=F32)
        yield
        state_ref[hh] = st * cd_ref[hh][0:1, :] + upd
        y = jnp.dot((inner * mask_ref[hh]).astype(BF16), v, preferred_element_type=F32) + cross
        yield
        mu = jnp.mean(y, axis=-1, keepdims=True)
        yc = y - mu
        var = jnp.mean(yc * yc, axis=-1, keepdims=True)
        yn = yc * lax.rsqrt(var + RET_GN_EPS) * gg_ref[:, cols] + gb_ref[:, cols]
        g = g_ref[rows, cols].astype(F32)
        o_ref[rows, cols] = (g * jax.nn.sigmoid(g) * yn).astype(o_ref.dtype)

    def body(c, carry):
        rows = pl.ds(pl.multiple_of(c * c_len, c_len), c_len)
        cos, sin = cos_ref[rows, :], sin_ref[rows, :]
        _round_robin([head_chunk(rows, hh, cos, sin) for hh in range(n_heads)])
        return carry

    lax.fori_loop(0, n_chunks, body, 0)


def _retention_core(qkvg, cos, sin, gn_g, gn_b, bn, t):
    d = qkvg.shape[1] // 4
    h = d // RET_HEAD_DIM
    c_len, dh = RET_CHUNK, RET_HEAD_DIM
    log_g = jnp.log1p(-(2.0 ** (-5.0 - jnp.arange(h, dtype=F32))))
    idx = jnp.arange(c_len, dtype=F32)
    diff = idx[:, None] - idx[None, :]
    mask = jnp.where(diff >= 0, jnp.exp(log_g[:, None, None] * jnp.maximum(diff, 0.0)), 0.0)
    q_decay = jnp.exp(log_g[:, None] * (idx + 1.0))
    k_decay = jnp.exp(log_g[:, None] * (c_len - 1.0 - idx))
    chunk_decay = jnp.exp(log_g * c_len)
    qd = jnp.broadcast_to(q_decay[:, :, None], (h, c_len, dh))
    kd = jnp.broadcast_to(k_decay[:, :, None], (h, c_len, dh))
    cd = jnp.broadcast_to(chunk_decay[:, None, None], (h, 8, dh))

    nh = min(RET_INTERLEAVE, h)
    bw = nh * dh
    rb = min(t, 1024)
    nr = t // rb
    hb = h // nh

    def col(off):
        return pl.BlockSpec((rb, bw), lambda b, j, i, off=off: (b * nr + i, off * hb + j))

    def per_head(rows, width):
        return pl.BlockSpec((nh, rows, width), lambda b, j, i: (j, 0, 0))

    tab = pl.BlockSpec((rb, dh // 2), lambda b, j, i: (b * nr + i, 0))
    vec = pl.BlockSpec((1, bw), lambda b, j, i: (0, j))
    return pl.pallas_call(
        functools.partial(_ret_kernel, n_chunks=rb // c_len, n_heads=nh),
        grid=(bn, hb, nr),
        in_specs=[col(0), col(1), col(2), col(3), tab, tab,
                  per_head(c_len, c_len), per_head(c_len, dh), per_head(c_len, dh), per_head(8, dh),
                  vec, vec],
        out_specs=pl.BlockSpec((rb, bw), lambda b, j, i: (b * nr + i, j)),
        out_shape=jax.ShapeDtypeStruct((bn * t, d), BF16),
        scratch_shapes=[pltpu.VMEM((nh, dh, dh), F32)],
        compiler_params=_cparams(("parallel", "parallel", "arbitrary"), VMEM_LIMIT),
        name="retention",
    )(qkvg, qkvg, qkvg, qkvg, cos, sin, mask, qd, kd, cd, gn_g.reshape(1, d), gn_b.reshape(1, d))


def _lora_act(t, act):
    if act == "tanh":
        return jnp.tanh(t)
    if act == "sigmoid":
        return jax.nn.sigmoid(t)
    return t


def _mix_kernel(x_ref, prev_ref, mu_ref, w1_ref, o_ref, t_ref, *, tiles_per_seq, loras):
    x = x_ref[...].astype(F32)
    first = (pl.program_id(0) % tiles_per_seq) == 0
    last = prev_ref.shape[0] - 1
    prev_row = jnp.where(first, 0.0, prev_ref[last:last + 1, :].astype(F32))
    shifted = pltpu.roll(x, 1, axis=0)
    rid = lax.broadcasted_iota(jnp.int32, x.shape, 0)
    xx = jnp.where(rid == 0, prev_row, shifted) - x
    for i in range(o_ref.shape[0]):
        o_ref[i] = (x + xx * mu_ref[i:i + 1, :]).astype(o_ref.dtype)
    for mix, act, lo, hi in loras:
        xm = (x + xx * mu_ref[mix:mix + 1, :]).astype(BF16)
        down = jnp.dot(xm, w1_ref[:, lo:hi], preferred_element_type=F32)
        t_ref[:, lo:hi] = _lora_act(down, act).astype(t_ref.dtype)


def _token_mix(x, mu, w1cat, loras, t):
    m, d = x.shape
    rt = w1cat.shape[1]
    tm = min(t, 256)
    pr = BF16_SUBLANES
    return pl.pallas_call(
        functools.partial(_mix_kernel, tiles_per_seq=t // tm, loras=loras),
        grid=(m // tm,),
        in_specs=[pl.BlockSpec((tm, d), lambda i: (i, 0)),
                  pl.BlockSpec((pr, d), lambda i: (jnp.maximum(i * (tm // pr) - 1, 0), 0)),
                  pl.BlockSpec((6, d), lambda i: (0, 0)),
                  pl.BlockSpec((d, rt), lambda i: (0, 0))],
        out_specs=[pl.BlockSpec((3, tm, d), lambda i: (0, i, 0)),
                   pl.BlockSpec((tm, rt), lambda i: (i, 0))],
        out_shape=[jax.ShapeDtypeStruct((3, m, d), BF16), jax.ShapeDtypeStruct((m, rt), BF16)],
        compiler_params=_cparams(("parallel",), VMEM_LIMIT),
        name="token_mix",
    )(x, x, mu, w1cat)


def _lora_up_kernel(t_ref, *refs, ranges):
    n = len(ranges)
    for (lo, hi), w2_ref, o_ref in zip(ranges, refs[:n], refs[n:]):
        o_ref[...] = jnp.dot(t_ref[:, lo:hi], w2_ref[...], preferred_element_type=F32).astype(o_ref.dtype)


def _pad_rank(r):
    return -(-r // LANES) * LANES


def _lora_prepare(specs):
    w1s, w2s, loras, col = [], [], [], 0
    for mix, w1, w2, act, _ in specs:
        r = w1.shape[1]
        rp = _pad_rank(r)
        w1s.append(jnp.pad(w1, ((0, 0), (0, rp - r))).astype(BF16))
        w2s.append(jnp.pad(w2, ((0, rp - r), (0, 0))).astype(BF16))
        loras.append((mix, act, col, col + rp))
        col += rp
    return jnp.concatenate(w1s, axis=1), w2s, tuple(loras)


def _lora_up(tact, w2s, loras, out_dtypes):
    m, rt = tact.shape
    d = w2s[0].shape[1]
    tm = min(m, 256)
    ranges = tuple((lo, hi) for _, _, lo, hi in loras)
    row = pl.BlockSpec((tm, d), lambda i: (i, 0))
    return pl.pallas_call(
        functools.partial(_lora_up_kernel, ranges=ranges),
        grid=(m // tm,),
        in_specs=[pl.BlockSpec((tm, rt), lambda i: (i, 0))]
                 + [pl.BlockSpec(w.shape, lambda i: (0, 0)) for w in w2s],
        out_specs=[row] * len(w2s),
        out_shape=[jax.ShapeDtypeStruct((m, d), dt) for dt in out_dtypes],
        compiler_params=_cparams(("parallel",), VMEM_LIMIT),
        name="lora_up",
    )(tact, *w2s)


def _split2(x):
    hi = x.astype(BF16)
    return hi, (x - hi.astype(F32)).astype(BF16)


def _block_diag(y, half_masks):
    yb = y.astype(BF16)
    zeros = jnp.zeros((y.shape[0], LANES), BF16)
    blocks = []
    for h in range(WKV_GROUP):
        col = h // 2
        part = yb[:, col * LANES:(col + 1) * LANES] * half_masks[h % 2]
        blocks.append(jnp.concatenate([part, zeros] if col == 0 else [zeros, part], axis=1))
    return jnp.concatenate(blocks, axis=0)


def _wkv_kernel(r_ref, k_ref, v_ref, lw_ref, la_ref, lg_ref, *rest, n_chunks, n_groups, vres):
    if vres:
        lv_ref, vf_ref = rest[0], rest[1]
        rest = rest[2:]
    (w0_ref, a0_ref, kk_ref, ka_ref, rk_ref, gg_ref, gb_ref, v0_ref,
     ones_ref, tri_ref, lo_ref, bdm_ref, o_ref, state_ref) = rest
    L, gw = WKV_CHUNK, MXU_DIM

    @pl.when(pl.program_id(2) == 0)
    def _():
        state_ref[...] = jnp.zeros_like(state_ref)

    lane = lax.broadcasted_iota(jnp.int32, (1, LANES), 1)
    half_masks = [(lane < RWKV_HEAD_DIM).astype(BF16), (lane >= RWKV_HEAD_DIM).astype(BF16)]

    def mm(a, b):
        return jnp.dot(a, b, preferred_element_type=F32)

    def seg_sum(x):
        return mm(x.astype(BF16), ones_ref[...])

    def bd(y):
        return _block_diag(y, half_masks)

    def group_chunk(rows, q):
        cols = slice(q * gw, (q + 1) * gw)
        r = r_ref[rows, cols].astype(F32)
        k0 = k_ref[rows, cols].astype(F32)
        v = v_ref[rows, cols].astype(F32)
        z = -(w0_ref[:, cols] + lw_ref[rows, cols])
        softplus = jnp.maximum(z, 0.0) + jnp.log1p(jnp.exp(-jnp.abs(z)))
        logw = -jnp.exp(-softplus - 0.5)
        a = jax.nn.sigmoid(a0_ref[:, cols] + la_ref[rows, cols].astype(F32))
        if vres:
            v = v + (vf_ref[rows, cols].astype(F32) - v) * jax.nn.sigmoid(
                v0_ref[:, cols] + lv_ref[rows, cols].astype(F32))
        kk = k0 * kk_ref[:, cols]
        ss = seg_sum(kk * kk)
        hi, lo = _split2(logw)
        cum = mm(tri_ref[...], hi) + mm(tri_ref[...], lo)
        yield
        kk = kk * lax.rsqrt(jnp.maximum(ss, 1e-24))
        k = k0 * (1.0 + (a - 1.0) * ka_ref[:, cols])
        avec, bvec = -kk, kk * a
        cum_l = cum[L - 1:L, :]
        e_out = jnp.exp(-cum)
        tail = jnp.exp(cum_l - cum)
        at = (avec * jnp.exp(cum - logw)).astype(BF16)
        rt = (r * jnp.exp(cum)).astype(BF16)
        ar = jnp.concatenate([at, rt], axis=0)
        bk = jnp.concatenate([bd(bvec * e_out), bd(k * e_out)], axis=0)
        p = lax.dot_general(ar, bk, (((1,), (1,)), ((), ())), preferred_element_type=F32)
        st = state_ref[q]
        ph = lax.dot_general(ar, st.astype(BF16), (((1,), (1,)), ((), ())),
                             preferred_element_type=F32)
        bonus_s = seg_sum(r * k * rk_ref[:, cols])
        yield
        strict, incl = lo_ref[0], lo_ref[1]
        m_ab = p[:L, :gw] * strict
        m_ak = p[:L, gw:] * strict
        n_ab = p[L:, :gw] * incl
        n_ak = p[L:, gw:] * incl
        pv = mm(jnp.concatenate([m_ak, n_ak], axis=0).astype(BF16), bd(v))
        x = m_ab
        x2 = mm(x.astype(BF16), bd(x))
        yield
        u = ph[:L] + pv[:L]
        for j in range(6):
            du = mm(x.astype(BF16), bd(u))
            if j < 5:
                x = x2
            if j < 4:
                x2 = mm(x.astype(BF16), bd(x))
            yield
            u = u + du

        yn_u = mm(n_ab.astype(BF16), bd(u))
        uv = jnp.concatenate([u, v], axis=0).astype(BF16)
        bkh = jnp.concatenate([bvec * tail, k * tail], axis=0).astype(BF16)
        upd = lax.dot_general(uv, bkh, (((0,), (0,)), ((), ())), preferred_element_type=F32)
        yield
        state_ref[q] = st * jnp.exp(cum_l) + upd * bdm_ref[...]
        y = ph[L:] + pv[L:] + yn_u
        mu = seg_sum(y) * (1.0 / RWKV_HEAD_DIM)
        yield
        yc = y - mu
        var = seg_sum(yc * yc) * (1.0 / RWKV_HEAD_DIM)
        yield
        yn = yc * lax.rsqrt(var + RWKV_GN_EPS) * gg_ref[:, cols] + gb_ref[:, cols]
        o_ref[rows, cols] = ((yn + bonus_s * v) * lg_ref[rows, cols].astype(F32)).astype(o_ref.dtype)

    def body(c, carry):
        rows = pl.ds(pl.multiple_of(c * L, L), L)
        _round_robin([group_chunk(rows, q) for q in range(n_groups)])
        return carry

    lax.fori_loop(0, n_chunks, body, 0)


def _wkv_core(rkv, lw, la, lg, lv, v_first, params, bn, t):
    _, m, d = rkv.shape
    L, gw = WKV_CHUNK, MXU_DIM
    n_groups = min(WKV_INTERLEAVE, d // gw)
    bw = n_groups * gw
    rb = min(t, 512)
    vres = lv is not None
    hid = jnp.arange(gw) // RWKV_HEAD_DIM
    bdm = (hid[:, None] == hid[None, :])
    ones_bd = bdm.astype(BF16)
    ti = jnp.arange(L)
    tri = (ti[:, None] >= ti[None, :]).astype(BF16)
    si = jnp.arange(gw) % L
    lo_masks = jnp.stack([(si[None, :] < ti[:, None]), (si[None, :] <= ti[:, None])]).astype(F32)

    def act(g=None):
        if g is None:
            return pl.BlockSpec((rb, bw), lambda b, q, i: (b * (t // rb) + i, q))
        return pl.BlockSpec((None, rb, bw), lambda b, q, i, g=g: (g, b * (t // rb) + i, q))

    vec = pl.BlockSpec((1, bw), lambda b, q, i: (0, q))

    def const(shape):
        return pl.BlockSpec(shape, lambda b, q, i: (0,) * len(shape))

    ins = [rkv, rkv, rkv, lw, la, lg]
    specs = [act(0), act(1), act(2), act(), act(), act()]
    if vres:
        ins += [lv, v_first]
        specs += [act(), act(2)]
    ins += [p.reshape(1, d) for p in params]
    specs += [vec] * len(params)
    ins += [ones_bd, tri, lo_masks, bdm.astype(F32)]
    specs += [const((gw, gw)), const((L, L)), const((2, L, gw)), const((gw, gw))]
    return pl.pallas_call(
        functools.partial(_wkv_kernel, n_chunks=rb // L, n_groups=n_groups, vres=vres),
        grid=(bn, d // bw, t // rb),
        in_specs=specs,
        out_specs=act(),
        out_shape=jax.ShapeDtypeStruct((m, d), BF16),
        scratch_shapes=[pltpu.VMEM((n_groups, gw, gw), F32)],
        compiler_params=_cparams(("parallel", "parallel", "arbitrary"), VMEM_LIMIT),
        name="wkv7",
    )(*ins)


def kernel(x, positions, ret_w_in, ret_gn_g, ret_gn_b, ret_w_o, rwkv_mu, rwkv_w_rkv, rwkv_w0, rwkv_w1, rwkv_w2, rwkv_a0, rwkv_a1, rwkv_a2, rwkv_g1, rwkv_g2, rwkv_k_k, rwkv_k_a, rwkv_r_k, rwkv_gn_g, rwkv_gn_b, rwkv_w_o, rwkv_v0, rwkv_v1, rwkv_v2, ln_mix_g, ln_mix_b, mlp_w1, mlp_w2, ln_mlp_g, ln_mlp_b):
    bn, t, d = x.shape
    m = bn * t
    depth = ln_mix_g.shape[0]
    alpha = (2 * depth) ** 0.25
    assert d % RET_HEAD_DIM == 0 and t % RET_CHUNK == 0 and t % WKV_CHUNK == 0

    xs = x.reshape(m, d).astype(F32)
    xb = xs.astype(BF16)
    cos, sin = _rope_tables(positions)
    mlp_w2_b = mlp_w2.astype(BF16)
    v_first = None
    mix_order = jnp.array([0, 2, 3, 1, 4, 5])

    for i in range(depth):
        j = i // N_MIXERS
        if i % N_MIXERS == 0:
            qkvg = _mm2d(xb, ret_w_in, j)
            gated = _retention_core(qkvg, cos, sin, ret_gn_g[j], ret_gn_b[j], bn, t)
            h = _mm2d(gated, ret_w_o, j)
        else:
            specs = [(3, rwkv_w1[j], rwkv_w2[j], "tanh", F32),
                     (4, rwkv_a1[j], rwkv_a2[j], None, BF16),
                     (5, rwkv_g1[j], rwkv_g2[j], "sigmoid", BF16)]
            if j > 0:
                specs.append((2, rwkv_v1[j - 1], rwkv_v2[j - 1], None, BF16))
            w1cat, w2s, loras = _lora_prepare(specs)
            x3, tact = _token_mix(xb, rwkv_mu[j][mix_order], w1cat, loras, t)
            rkv = _matmul(x3, rwkv_w_rkv.reshape(-1, d, d), g=3, w_off=3 * j)
            ups = _lora_up(tact, w2s, loras, [s[4] for s in specs])
            lw, la, lg = ups[:3]
            if j == 0:
                lv, v0 = None, jnp.zeros((d,), F32)
                v_first = rkv
            else:
                lv, v0 = ups[3], rwkv_v0[j - 1]
            params = (rwkv_w0[j], rwkv_a0[j], rwkv_k_k[j], rwkv_k_a[j], rwkv_r_k[j].reshape(d),
                      rwkv_gn_g[j], rwkv_gn_b[j], v0)
            gated = _wkv_core(rkv, lw, la, lg, lv, v_first, params, bn, t)
            h = _mm2d(gated, rwkv_w_o, j)
        xs = xb = _add_ln(xs, h, ln_mix_g[i], ln_mix_b[i], alpha, BF16)
        hid = _mm2d(xb, mlp_w1, i, act="relu2")
        h = _matmul_long_k(hid, mlp_w2_b, i)
        xs = xb = _add_ln(xs, h, ln_mlp_g[i], ln_mlp_b[i], alpha, BF16 if i + 1 < depth else F32)
    return xs.reshape(bn, t, d).astype(x.dtype)
```

```python
import functools

import jax
import jax.numpy as jnp
from jax import lax
from jax.experimental import pallas as pl
from jax.experimental.pallas import tpu as pltpu

F32 = jnp.float32
BF16 = jnp.bfloat16

RET_HEAD_DIM = 256
RET_CHUNK = 128
ROPE_BASE = 10000.0
RET_GN_EPS = 1e-5
RWKV_HEAD_DIM = 64
RWKV_GN_EPS = 64e-5
LN_EPS = 1e-5
N_MIXERS = 2

LANES = 128
BF16_SUBLANES = 16
MXU_DIM = 256
WKV_CHUNK = 64
WKV_GROUP = MXU_DIM // RWKV_HEAD_DIM
WKV_INTERLEAVE = 8
RET_INTERLEAVE = 8
VMEM_LIMIT = 56 * 1024 * 1024


def _cparams(sem, vmem=None):
    return pltpu.CompilerParams(dimension_semantics=sem, vmem_limit_bytes=vmem)


def _round_robin(stage_generators):
    live = list(stage_generators)
    while live:
        nxt = []
        for g in live:
            try:
                next(g)
                nxt.append(g)
            except StopIteration:
                pass
        live = nxt


def _act(x, act):
    if act == "relu2":
        return jnp.square(jnp.maximum(x, 0.0))
    return x


def _mm_kernel(x_ref, w_ref, o_ref, *, act):
    part = jnp.dot(x_ref[...], w_ref[...].astype(BF16), preferred_element_type=F32)
    o_ref[...] = _act(part, act).astype(o_ref.dtype)


def _mm_kt_kernel(x_ref, w_ref, o_ref, acc_ref, *, nk, act):
    part = jnp.dot(x_ref[...], w_ref[...], preferred_element_type=F32)
    k = pl.program_id(2)

    @pl.when(k == 0)
    def _():
        acc_ref[...] = part

    @pl.when(k > 0)
    def _():
        acc_ref[...] += part

    @pl.when(k == nk - 1)
    def _():
        o_ref[...] = _act(acc_ref[...], act).astype(o_ref.dtype)


def _matmul(x, w, *, g=1, w_off=0, act=None, out_dtype=BF16):
    _, m, kd = x.shape
    _, _, n = w.shape
    tm, tn = min(m, 1024), min(n, 512)
    return pl.pallas_call(
        functools.partial(_mm_kernel, act=act),
        grid=(g, m // tm, n // tn),
        in_specs=[pl.BlockSpec((None, tm, kd), lambda b, i, j: (b, i, 0)),
                  pl.BlockSpec((None, kd, tn), lambda b, i, j: (w_off + b, 0, j))],
        out_specs=pl.BlockSpec((None, tm, tn), lambda b, i, j: (b, i, j)),
        out_shape=jax.ShapeDtypeStruct((g, m, n), out_dtype),
        compiler_params=_cparams(("parallel", "parallel", "parallel"), VMEM_LIMIT),
        name="matmul",
    )(x, w)


def _matmul_long_k(x, w_stack, layer, *, act=None, out_dtype=BF16):
    m, kd = x.shape
    n = w_stack.shape[2]
    tm, tn, tk = min(m, 1024), min(n, 1024), min(kd, 4096)
    nk = kd // tk
    return pl.pallas_call(
        functools.partial(_mm_kt_kernel, nk=nk, act=act),
        grid=(m // tm, n // tn, nk),
        in_specs=[pl.BlockSpec((tm, tk), lambda i, j, k: (i, k)),
                  pl.BlockSpec((None, tk, tn), lambda i, j, k: (layer, k, j))],
        out_specs=pl.BlockSpec((tm, tn), lambda i, j, k: (i, j)),
        out_shape=jax.ShapeDtypeStruct((m, n), out_dtype),
        scratch_shapes=[pltpu.VMEM((tm, tn), F32)],
        compiler_params=_cparams(("parallel", "parallel", "arbitrary"), VMEM_LIMIT),
        name="matmul_long_k",
    )(x, w_stack)


def _mm2d(x, w_stack, layer, **kw):
    return _matmul(x[None], w_stack, w_off=layer, **kw)[0]


def _add_ln_kernel(x_ref, h_ref, g_ref, b_ref, o_ref, *, alpha):
    z = alpha * x_ref[...].astype(F32) + h_ref[...].astype(F32)
    mu = jnp.mean(z, axis=-1, keepdims=True)
    zc = z - mu
    var = jnp.mean(zc * zc, axis=-1, keepdims=True)
    y = zc * lax.rsqrt(var + LN_EPS) * g_ref[...] + b_ref[...]
    o_ref[...] = y.astype(o_ref.dtype)


def _add_ln(x, h, g, b, alpha, out_dtype):
    m, d = x.shape
    tm = min(m, 256)
    row = pl.BlockSpec((tm, d), lambda i: (i, 0))
    vec = pl.BlockSpec((1, d), lambda i: (0, 0))
    return pl.pallas_call(
        functools.partial(_add_ln_kernel, alpha=alpha),
        grid=(m // tm,),
        in_specs=[row, row, vec, vec],
        out_specs=row,
        out_shape=jax.ShapeDtypeStruct((m, d), out_dtype),
        compiler_params=_cparams(("parallel",), VMEM_LIMIT),
        name="add_ln",
    )(x, h, g.reshape(1, d), b.reshape(1, d))


def _rope_kernel(pos_ref, freq_ref, cos_ref, sin_ref):
    ang = pos_ref[...].astype(F32) * freq_ref[...]
    cos_ref[...] = jnp.cos(ang)
    sin_ref[...] = jnp.sin(ang)


def _rope_tables(positions):
    m = positions.size
    half = RET_HEAD_DIM // 2
    inv_freq = ROPE_BASE ** (-jnp.arange(half, dtype=F32) / half)
    tm = min(m, 512)
    out = pl.BlockSpec((tm, half), lambda i: (i, 0))
    return pl.pallas_call(
        _rope_kernel,
        grid=(m // tm,),
        in_specs=[pl.BlockSpec((tm, 1), lambda i: (i, 0)),
                  pl.BlockSpec((1, half), lambda i: (0, 0))],
        out_specs=[out, out],
        out_shape=[jax.ShapeDtypeStruct((m, half), F32)] * 2,
        compiler_params=_cparams(("parallel",)),
        name="rope_tables",
    )(positions.reshape(m, 1), inv_freq.reshape(1, half))


def _ret_kernel(q_ref, k_ref, v_ref, g_ref, cos_ref, sin_ref, mask_ref, qd_ref, kd_ref, cd_ref,
                gg_ref, gb_ref, o_ref, state_ref, *, n_chunks, n_heads):
    c_len, dh = RET_CHUNK, RET_HEAD_DIM
    half = dh // 2

    @pl.when(pl.program_id(2) == 0)
    def _():
        state_ref[...] = jnp.zeros_like(state_ref)

    def rot(t, cos, sin):
        t1, t2 = t[:, :half], t[:, half:]
        return jnp.concatenate([t1 * cos - t2 * sin, t1 * sin + t2 * cos], axis=-1)

    def head_chunk(rows, hh, cos, sin):
        cols = slice(hh * dh, (hh + 1) * dh)
        q = rot(q_ref[rows, cols].astype(F32), cos, sin)
        k = rot(k_ref[rows, cols].astype(F32), cos, sin) * (dh ** -0.5)
        v = v_ref[rows, cols]
        inner = lax.dot_general(q.astype(BF16), k.astype(BF16), (((1,), (1,)), ((), ())),
                                preferred_element_type=F32)
        st = state_ref[hh]
        cross = jnp.dot((q * qd_ref[hh]).astype(BF16), st.astype(BF16), preferred_element_type=F32)
        kdt = jnp.transpose(k * kd_ref[hh]).astype(BF16)
        upd = jnp.dot(kdt, v, preferred_element_type=F32)
        yield
        state_ref[hh] = st * cd_ref[hh][0:1, :] + upd
        y = jnp.dot((inner * mask_ref[hh]).astype(BF16), v, preferred_element_type=F32) + cross
        yield
        mu = jnp.mean(y, axis=-1, keepdims=True)
        yc = y - mu
        var = jnp.mean(yc * yc, axis=-1, keepdims=True)
        yn = yc * lax.rsqrt(var + RET_GN_EPS) * gg_ref[:, cols] + gb_ref[:, cols]
        g = g_ref[rows, cols].astype(F32)
        o_ref[rows, cols] = (g * jax.nn.sigmoid(g) * yn).astype(o_ref.dtype)

    def body(c, carry):
        rows = pl.ds(pl.multiple_of(c * c_len, c_len), c_len)
        cos, sin = cos_ref[rows, :], sin_ref[rows, :]
        _round_robin([head_chunk(rows, hh, cos, sin) for hh in range(n_heads)])
        return carry

    lax.fori_loop(0, n_chunks, body, 0)


def _retention_core(qkvg, cos, sin, gn_g, gn_b, bn, t):
    d = qkvg.shape[1] // 4
    h = d // RET_HEAD_DIM
    c_len, dh = RET_CHUNK, RET_HEAD_DIM
    log_g = jnp.log1p(-(2.0 ** (-5.0 - jnp.arange(h, dtype=F32))))
    idx = jnp.arange(c_len, dtype=F32)
    diff = idx[:, None] - idx[None, :]
    mask = jnp.where(diff >= 0, jnp.exp(log_g[:, None, None] * jnp.maximum(diff, 0.0)), 0.0)
    q_decay = jnp.exp(log_g[:, None] * (idx + 1.0))
    k_decay = jnp.exp(log_g[:, None] * (c_len - 1.0 - idx))
    chunk_decay = jnp.exp(log_g * c_len)
    qd = jnp.broadcast_to(q_decay[:, :, None], (h, c_len, dh))
    kd = jnp.broadcast_to(k_decay[:, :, None], (h, c_len, dh))
    cd = jnp.broadcast_to(chunk_decay[:, None, None], (h, 8, dh))

    nh = min(RET_INTERLEAVE, h)
    bw = nh * dh
    rb = min(t, 1024)
    nr = t // rb
    hb = h // nh

    def col(off):
        return pl.BlockSpec((rb, bw), lambda b, j, i, off=off: (b * nr + i, off * hb + j))

    def per_head(rows, width):
        return pl.BlockSpec((nh, rows, width), lambda b, j, i: (j, 0, 0))

    tab = pl.BlockSpec((rb, dh // 2), lambda b, j, i: (b * nr + i, 0))
    vec = pl.BlockSpec((1, bw), lambda b, j, i: (0, j))
    return pl.pallas_call(
        functools.partial(_ret_kernel, n_chunks=rb // c_len, n_heads=nh),
        grid=(bn, hb, nr),
        in_specs=[col(0), col(1), col(2), col(3), tab, tab,
                  per_head(c_len, c_len), per_head(c_len, dh), per_head(c_len, dh), per_head(8, dh),
                  vec, vec],
        out_specs=pl.BlockSpec((rb, bw), lambda b, j, i: (b * nr + i, j)),
        out_shape=jax.ShapeDtypeStruct((bn * t, d), BF16),
        scratch_shapes=[pltpu.VMEM((nh, dh, dh), F32)],
        compiler_params=_cparams(("parallel", "parallel", "arbitrary"), VMEM_LIMIT),
        name="retention",
    )(qkvg, qkvg, qkvg, qkvg, cos, sin, mask, qd, kd, cd, gn_g.reshape(1, d), gn_b.reshape(1, d))


def _lora_act(t, act):
    if act == "tanh":
        return jnp.tanh(t)
    if act == "sigmoid":
        return jax.nn.sigmoid(t)
    return t


def _mix_kernel(x_ref, prev_ref, mu_ref, w1_ref, o_ref, t_ref, *, tiles_per_seq, loras):
    x = x_ref[...].astype(F32)
    first = (pl.program_id(0) % tiles_per_seq) == 0
    last = prev_ref.shape[0] - 1
    prev_row = jnp.where(first, 0.0, prev_ref[last:last + 1, :].astype(F32))
    shifted = pltpu.roll(x, 1, axis=0)
    rid = lax.broadcasted_iota(jnp.int32, x.shape, 0)
    xx = jnp.where(rid == 0, prev_row, shifted) - x
    for i in range(o_ref.shape[0]):
        o_ref[i] = (x + xx * mu_ref[i:i + 1, :]).astype(o_ref.dtype)
    for mix, act, lo, hi in loras:
        xm = (x + xx * mu_ref[mix:mix + 1, :]).astype(BF16)
        down = jnp.dot(xm, w1_ref[:, lo:hi], preferred_element_type=F32)
        t_ref[:, lo:hi] = _lora_act(down, act).astype(t_ref.dtype)


def _token_mix(x, mu, w1cat, loras, t):
    m, d = x.shape
    rt = w1cat.shape[1]
    tm = min(t, 256)
    pr = BF16_SUBLANES
    return pl.pallas_call(
        functools.partial(_mix_kernel, tiles_per_seq=t // tm, loras=loras),
        grid=(m // tm,),
        in_specs=[pl.BlockSpec((tm, d), lambda i: (i, 0)),
                  pl.BlockSpec((pr, d), lambda i: (jnp.maximum(i * (tm // pr) - 1, 0), 0)),
                  pl.BlockSpec((6, d), lambda i: (0, 0)),
                  pl.BlockSpec((d, rt), lambda i: (0, 0))],
        out_specs=[pl.BlockSpec((3, tm, d), lambda i: (0, i, 0)),
                   pl.BlockSpec((tm, rt), lambda i: (i, 0))],
        out_shape=[jax.ShapeDtypeStruct((3, m, d), BF16), jax.ShapeDtypeStruct((m, rt), BF16)],
        compiler_params=_cparams(("parallel",), VMEM_LIMIT),
        name="token_mix",
    )(x, x, mu, w1cat)


def _lora_up_kernel(t_ref, *refs, ranges):
    n = len(ranges)
    for (lo, hi), w2_ref, o_ref in zip(ranges, refs[:n], refs[n:]):
        o_ref[...] = jnp.dot(t_ref[:, lo:hi], w2_ref[...], preferred_element_type=F32).astype(o_ref.dtype)


def _pad_rank(r):
    return -(-r // LANES) * LANES


def _lora_prepare(specs):
    w1s, w2s, loras, col = [], [], [], 0
    for mix, w1, w2, act, _ in specs:
        r = w1.shape[1]
        rp = _pad_rank(r)
        w1s.append(jnp.pad(w1, ((0, 0), (0, rp - r))).astype(BF16))
        w2s.append(jnp.pad(w2, ((0, rp - r), (0, 0))).astype(BF16))
        loras.append((mix, act, col, col + rp))
        col += rp
    return jnp.concatenate(w1s, axis=1), w2s, tuple(loras)


def _lora_up(tact, w2s, loras, out_dtypes):
    m, rt = tact.shape
    d = w2s[0].shape[1]
    tm = min(m, 256)
    ranges = tuple((lo, hi) for _, _, lo, hi in loras)
    row = pl.BlockSpec((tm, d), lambda i: (i, 0))
    return pl.pallas_call(
        functools.partial(_lora_up_kernel, ranges=ranges),
        grid=(m // tm,),
        in_specs=[pl.BlockSpec((tm, rt), lambda i: (i, 0))]
                 + [pl.BlockSpec(w.shape, lambda i: (0, 0)) for w in w2s],
        out_specs=[row] * len(w2s),
        out_shape=[jax.ShapeDtypeStruct((m, d), dt) for dt in out_dtypes],
        compiler_params=_cparams(("parallel",), VMEM_LIMIT),
        name="lora_up",
    )(tact, *w2s)


def _split2(x):
    hi = x.astype(BF16)
    return hi, (x - hi.astype(F32)).astype(BF16)


def _block_diag(y, half_masks):
    yb = y.astype(BF16)
    zeros = jnp.zeros((y.shape[0], LANES), BF16)
    blocks = []
    for h in range(WKV_GROUP):
        col = h // 2
        part = yb[:, col * LANES:(col + 1) * LANES] * half_masks[h % 2]
        blocks.append(jnp.concatenate([part, zeros] if col == 0 else [zeros, part], axis=1))
    return jnp.concatenate(blocks, axis=0)


def _wkv_kernel(r_ref, k_ref, v_ref, lw_ref, la_ref, lg_ref, *rest, n_chunks, n_groups, vres):
    if vres:
        lv_ref, vf_ref = rest[0], rest[1]
        rest = rest[2:]
    (w0_ref, a0_ref, kk_ref, ka_ref, rk_ref, gg_ref, gb_ref, v0_ref,
     ones_ref, tri_ref, lo_ref, bdm_ref, o_ref, state_ref) = rest
    L, gw = WKV_CHUNK, MXU_DIM

    @pl.when(pl.program_id(2) == 0)
    def _():
        state_ref[...] = jnp.zeros_like(state_ref)

    lane = lax.broadcasted_iota(jnp.int32, (1, LANES), 1)
    half_masks = [(lane < RWKV_HEAD_DIM).astype(BF16), (lane >= RWKV_HEAD_DIM).astype(BF16)]

    def mm(a, b):
        return jnp.dot(a, b, preferred_element_type=F32)

    def seg_sum(x):
        return mm(x.astype(BF16), ones_ref[...])

    def bd(y):
        return _block_diag(y, half_masks)

    def group_chunk(rows, q):
        cols = slice(q * gw, (q + 1) * gw)
        r = r_ref[rows, cols].astype(F32)
        k0 = k_ref[rows, cols].astype(F32)
        v = v_ref[rows, cols].astype(F32)
        z = -(w0_ref[:, cols] + lw_ref[rows, cols])
        softplus = jnp.maximum(z, 0.0) + jnp.log1p(jnp.exp(-jnp.abs(z)))
        logw = -jnp.exp(-softplus - 0.5)
        a = jax.nn.sigmoid(a0_ref[:, cols] + la_ref[rows, cols].astype(F32))
        if vres:
            v = v + (vf_ref[rows, cols].astype(F32) - v) * jax.nn.sigmoid(
                v0_ref[:, cols] + lv_ref[rows, cols].astype(F32))
        kk = k0 * kk_ref[:, cols]
        ss = seg_sum(kk * kk)
        hi, lo = _split2(logw)
        cum = mm(tri_ref[...], hi) + mm(tri_ref[...], lo)
        yield
        kk = kk * lax.rsqrt(jnp.maximum(ss, 1e-24))
        k = k0 * (1.0 + (a - 1.0) * ka_ref[:, cols])
        avec, bvec = -kk, kk * a
        cum_l = cum[L - 1:L, :]
        e_out = jnp.exp(-cum)
        tail = jnp.exp(cum_l - cum)
        at = (avec * jnp.exp(cum - logw)).astype(BF16)
        rt = (r * jnp.exp(cum)).astype(BF16)
        ar = jnp.concatenate([at, rt], axis=0)
        bk = jnp.concatenate([bd(bvec * e_out), bd(k * e_out)], axis=0)
        p = lax.dot_general(ar, bk, (((1,), (1,)), ((), ())), preferred_element_type=F32)
        st = state_ref[q]
        ph = lax.dot_general(ar, st.astype(BF16), (((1,), (1,)), ((), ())),
                             preferred_element_type=F32)
        bonus_s = seg_sum(r * k * rk_ref[:, cols])
        yield
        strict, incl = lo_ref[0], lo_ref[1]
        m_ab = p[:L, :gw] * strict
        m_ak = p[:L, gw:] * strict
        n_ab = p[L:, :gw] * incl
        n_ak = p[L:, gw:] * incl
        pv = mm(jnp.concatenate([m_ak, n_ak], axis=0).astype(BF16), bd(v))
        x = m_ab
        x2 = mm(x.astype(BF16), bd(x))
        yield
        u = ph[:L] + pv[:L]
        for j in range(6):
            du = mm(x.astype(BF16), bd(u))
            if j < 5:
                x = x2
            if j < 4:
                x2 = mm(x.astype(BF16), bd(x))
            yield
            u = u + du

        yn_u = mm(n_ab.astype(BF16), bd(u))
        uv = jnp.concatenate([u, v], axis=0).astype(BF16)
        bkh = jnp.concatenate([bvec * tail, k * tail], axis=0).astype(BF16)
        upd = lax.dot_general(uv, bkh, (((0,), (0,)), ((), ())), preferred_element_type=F32)
        yield
        state_ref[q] = st * jnp.exp(cum_l) + upd * bdm_ref[...]
        y = ph[L:] + pv[L:] + yn_u
        mu = seg_sum(y) * (1.0 / RWKV_HEAD_DIM)
        yield
        yc = y - mu
        var = seg_sum(yc * yc) * (1.0 / RWKV_HEAD_DIM)
        yield
        yn = yc * lax.rsqrt(var + RWKV_GN_EPS) * gg_ref[:, cols] + gb_ref[:, cols]
        o_ref[rows, cols] = ((yn + bonus_s * v) * lg_ref[rows, cols].astype(F32)).astype(o_ref.dtype)

    def body(c, carry):
        rows = pl.ds(pl.multiple_of(c * L, L), L)
        _round_robin([group_chunk(rows, q) for q in range(n_groups)])
        return carry

    lax.fori_loop(0, n_chunks, body, 0)


def _wkv_core(rkv, lw, la, lg, lv, v_first, params, bn, t):
    _, m, d = rkv.shape
    L, gw = WKV_CHUNK, MXU_DIM
    n_groups = min(WKV_INTERLEAVE, d // gw)
    bw = n_groups * gw
    rb = min(t, 256)
    vres = lv is not None
    hid = jnp.arange(gw) // RWKV_HEAD_DIM
    bdm = (hid[:, None] == hid[None, :])
    ones_bd = bdm.astype(BF16)
    ti = jnp.arange(L)
    tri = (ti[:, None] >= ti[None, :]).astype(BF16)
    si = jnp.arange(gw) % L
    lo_masks = jnp.stack([(si[None, :] < ti[:, None]), (si[None, :] <= ti[:, None])]).astype(F32)

    def act(g=None):
        if g is None:
            return pl.BlockSpec((rb, bw), lambda b, q, i: (b * (t // rb) + i, q))
        return pl.BlockSpec((None, rb, bw), lambda b, q, i, g=g: (g, b * (t // rb) + i, q))

    vec = pl.BlockSpec((1, bw), lambda b, q, i: (0, q))

    def const(shape):
        return pl.BlockSpec(shape, lambda b, q, i: (0,) * len(shape))

    ins = [rkv, rkv, rkv, lw, la, lg]
    specs = [act(0), act(1), act(2), act(), act(), act()]
    if vres:
        ins += [lv, v_first]
        specs += [act(), act(2)]
    ins += [p.reshape(1, d) for p in params]
    specs += [vec] * len(params)
    ins += [ones_bd, tri, lo_masks, bdm.astype(F32)]
    specs += [const((gw, gw)), const((L, L)), const((2, L, gw)), const((gw, gw))]
    return pl.pallas_call(
        functools.partial(_wkv_kernel, n_chunks=rb // L, n_groups=n_groups, vres=vres),
        grid=(bn, d // bw, t // rb),
        in_specs=specs,
        out_specs=act(),
        out_shape=jax.ShapeDtypeStruct((m, d), BF16),
        scratch_shapes=[pltpu.VMEM((n_groups, gw, gw), F32)],
        compiler_params=_cparams(("parallel", "parallel", "arbitrary"), VMEM_LIMIT),
        name="wkv7",
    )(*ins)


def kernel(x, positions, ret_w_in, ret_gn_g, ret_gn_b, ret_w_o, rwkv_mu, rwkv_w_rkv, rwkv_w0, rwkv_w1, rwkv_w2, rwkv_a0, rwkv_a1, rwkv_a2, rwkv_g1, rwkv_g2, rwkv_k_k, rwkv_k_a, rwkv_r_k, rwkv_gn_g, rwkv_gn_b, rwkv_w_o, rwkv_v0, rwkv_v1, rwkv_v2, ln_mix_g, ln_mix_b, mlp_w1, mlp_w2, ln_mlp_g, ln_mlp_b):
    bn, t, d = x.shape
    m = bn * t
    depth = ln_mix_g.shape[0]
    alpha = (2 * depth) ** 0.25
    assert d % RET_HEAD_DIM == 0 and t % RET_CHUNK == 0 and t % WKV_CHUNK == 0

    xs = x.reshape(m, d).astype(F32)
    xb = xs.astype(BF16)
    cos, sin = _rope_tables(positions)
    mlp_w2_b = mlp_w2.astype(BF16)
    v_first = None
    mix_order = jnp.array([0, 2, 3, 1, 4, 5])

    for i in range(depth):
        j = i // N_MIXERS
        if i % N_MIXERS == 0:
            qkvg = _mm2d(xb, ret_w_in, j)
            gated = _retention_core(qkvg, cos, sin, ret_gn_g[j], ret_gn_b[j], bn, t)
            h = _mm2d(gated, ret_w_o, j)
        else:
            specs = [(3, rwkv_w1[j], rwkv_w2[j], "tanh", F32),
                     (4, rwkv_a1[j], rwkv_a2[j], None, BF16),
                     (5, rwkv_g1[j], rwkv_g2[j], "sigmoid", BF16)]
            if j > 0:
                specs.append((2, rwkv_v1[j - 1], rwkv_v2[j - 1], None, BF16))
            w1cat, w2s, loras = _lora_prepare(specs)
            x3, tact = _token_mix(xb, rwkv_mu[j][mix_order], w1cat, loras, t)
            rkv = _matmul(x3, rwkv_w_rkv.reshape(-1, d, d), g=3, w_off=3 * j)
            ups = _lora_up(tact, w2s, loras, [s[4] for s in specs])
            lw, la, lg = ups[:3]
            if j == 0:
                lv, v0 = None, jnp.zeros((d,), F32)
                v_first = rkv
            else:
                lv, v0 = ups[3], rwkv_v0[j - 1]
            params = (rwkv_w0[j], rwkv_a0[j], rwkv_k_k[j], rwkv_k_a[j], rwkv_r_k[j].reshape(d),
                      rwkv_gn_g[j], rwkv_gn_b[j], v0)
            gated = _wkv_core(rkv, lw, la, lg, lv, v_first, params, bn, t)
            h = _mm2d(gated, rwkv_w_o, j)
        xs = xb = _add_ln(xs, h, ln_mix_g[i], ln_mix_b[i], alpha, BF16)
        hid = _mm2d(xb, mlp_w1, i, act="relu2")
        h = _matmul_long_k(hid, mlp_w2_b, i)
        xs = xb = _add_ln(xs, h, ln_mlp_g[i], ln_mlp_b[i], alpha, BF16 if i + 1 < depth else F32)
    return xs.reshape(bn, t, d).astype(x.dtype)
```
